```python
import math
import jax, jax.numpy as jnp
from jax import lax
import numpy as np

D_MODEL = 2048
BATCH = 4
SEQ = 4096
DEPTH = 2

GRID_W = 64
CTX_LEN = 256
NORM_EPS = 1e-6
N_MOD = 6
F32 = jnp.float32

GLA_HEADS = 4
GLA_DK = D_MODEL // 16
GLA_DV = D_MODEL // 8
GLA_K_WIDTH = GLA_HEADS * GLA_DK
GLA_V_WIDTH = GLA_HEADS * GLA_DV
GLA_GATE_RANK = 16
GLA_GATE_NORMALIZER = 16.0
GLA_CHUNK = 64
CONV_CH = D_MODEL - GLA_V_WIDTH
CONV_WIDTH = 31
EVEN_SIZES = (GLA_K_WIDTH, GLA_K_WIDTH, GLA_V_WIDTH, GLA_V_WIDTH, GLA_GATE_RANK, GLA_GATE_RANK, CONV_CH, CONV_CH)
EVEN_IN = sum(EVEN_SIZES)

RWKV_HEAD = 64
RWKV_WIDTH = 3 * D_MODEL // 4
RWKV_HEADS = RWKV_WIDTH // RWKV_HEAD
RWKV_DECAY_RANK = 64
RWKV_ICLR_RANK = 64
RWKV_GATE_RANK = 128
RWKV_GN_EPS = 64e-5
RWKV_SHIFT_SIZES = (RWKV_WIDTH, RWKV_WIDTH, RWKV_WIDTH, RWKV_DECAY_RANK, RWKV_ICLR_RANK)
RWKV_SHIFT_WIDTH = sum(RWKV_SHIFT_SIZES)
S5_WIDTH = D_MODEL - RWKV_WIDTH
S5_GROUP = 16
S5_GROUPS = S5_WIDTH // S5_GROUP
S5_STATE = 64
S5_MAX_RE = -1e-4
ODD_SIZES = (RWKV_WIDTH, RWKV_WIDTH, RWKV_WIDTH, RWKV_DECAY_RANK, RWKV_DECAY_RANK, RWKV_ICLR_RANK, RWKV_ICLR_RANK, RWKV_GATE_RANK, S5_WIDTH)
ODD_IN = sum(ODD_SIZES)

N_EXPERTS = 32
TOP_K = 4
D_EXPERT = D_MODEL
SWIGLU_ALPHA = 1.702
SWIGLU_LIMIT = 7.0

N_EVEN = (DEPTH + 1) // 2
N_ODD = DEPTH // 2

kernel_name = 'hybrid_gla_conformer_rwkv7_s5_moe_dit'


def rms_norm(x, g):
    xf = x.astype(F32)
    y = xf * lax.rsqrt(jnp.mean(xf * xf, axis=-1, keepdims=True) + NORM_EPS)
    return (y * g.astype(F32)).astype(x.dtype)


def modulate(h, shift, scale):
    return h * (1.0 + scale) + shift


def split_cols(z, sizes):
    return jnp.split(z, np.cumsum(sizes)[:-1].tolist(), axis=-1)


def flip_t(a):
    return a[:, ::-1]


def token_shift(a):
    return jnp.pad(a, ((0, 0), (1, 0), (0, 0)))[:, :-1]


def to_col_major(a):
    b, t, ch = a.shape
    rows = t // GRID_W
    return a.reshape(b, rows, GRID_W, ch).transpose(0, 2, 1, 3).reshape(b, t, ch)


def from_col_major(a):
    b, t, ch = a.shape
    rows = t // GRID_W
    return a.reshape(b, GRID_W, rows, ch).transpose(0, 2, 1, 3).reshape(b, t, ch)


def gla_chunk_scan(q, k, v, log_a, s0, emit):
    bsz, t, h, _ = q.shape
    n = t // GLA_CHUNK

    def chunks(a):
        return a.reshape(bsz, n, GLA_CHUNK, h, a.shape[-1]).transpose(1, 0, 3, 2, 4)

    qc, kc, vc, gc = chunks(q), chunks(k), chunks(v), chunks(log_a)
    cum = jnp.cumsum(gc, axis=3)
    last = cum[:, :, :, -1:, :]
    k_state = kc * jnp.exp(last - cum)
    chunk_decay = jnp.exp(last[:, :, :, 0, :])
    if not emit:
        def state_step(s, xs):
            k_i, v_i, dec = xs
            return s * dec[..., None] + jnp.einsum('bhld,bhlv->bhdv', k_i, v_i), None
        s_fin, _ = lax.scan(state_step, s0, (k_state, vc, chunk_decay))
        return None, s_fin
    q_dec = qc * jnp.exp(cum)
    k_inv = kc * jnp.exp(-cum)
    lower_tri = jnp.tril(jnp.ones((GLA_CHUNK, GLA_CHUNK), dtype=bool))
    scores = jnp.where(lower_tri, jnp.einsum('nbhld,nbhmd->nbhlm', q_dec, k_inv), 0.0)
    o_intra = jnp.einsum('nbhlm,nbhmv->nbhlv', scores, vc)

    def step(s, xs):
        q_i, k_i, v_i, dec = xs
        o_i = jnp.einsum('bhld,bhdv->bhlv', q_i, s)
        return s * dec[..., None] + jnp.einsum('bhld,bhlv->bhdv', k_i, v_i), o_i

    s_fin, o_inter = lax.scan(step, s0, (q_dec, k_state, vc, chunk_decay))
    o = (o_intra + o_inter).transpose(1, 0, 3, 2, 4).reshape(bsz, t, h, -1)
    return o, s_fin


def gla_bidir(q, k, v, g_lr_f, g_lr_b, gate_w, gate_b, s0_f, s0_b, emit):
    bsz, t, _ = q.shape

    def heads(a, d):
        return a.astype(F32).reshape(bsz, t, GLA_HEADS, d)

    def log_decay(g_lr, d):
        zg = g_lr.astype(F32) @ gate_w[d].astype(F32) + gate_b[d].astype(F32)
        return heads(jax.nn.log_sigmoid(zg) / GLA_GATE_NORMALIZER, GLA_DK)

    qh = heads(q, GLA_DK) * GLA_DK ** -0.5
    kh = heads(k, GLA_DK)
    vh = heads(v, GLA_DV)
    o_f, s_f = gla_chunk_scan(qh, kh, vh, log_decay(g_lr_f, 0), s0_f, emit)
    o_b, s_b = gla_chunk_scan(flip_t(qh), flip_t(kh), flip_t(vh), flip_t(log_decay(g_lr_b, 1)), s0_b, emit)
    o = o_f + flip_t(o_b) if emit else None
    return o, s_f, s_b


def head_rms_norm(o, g):
    y = o * lax.rsqrt(jnp.mean(o * o, axis=-1, keepdims=True) + NORM_EPS) * g.astype(F32)
    return y.reshape(o.shape[0], o.shape[1], -1)


def conformer_conv(ca, cb, conv_w, conv_b, ln_g, ln_b, on_grid):
    h = ca * jax.nn.sigmoid(cb)
    bsz, t, ch = h.shape
    if on_grid:
        h = h.reshape(bsz * (t // GRID_W), GRID_W, ch)
    h = lax.conv_general_dilated(h, conv_w[:, None, :].astype(h.dtype), window_strides=(1,),
                                 padding=[(CONV_WIDTH // 2, CONV_WIDTH // 2)],
                                 dimension_numbers=('NWC', 'WIO', 'NWC'), feature_group_count=ch)
    hf = (h.reshape(bsz, t, ch) + conv_b).astype(F32)
    mean = jnp.mean(hf, axis=-1, keepdims=True)
    var = jnp.mean(jnp.square(hf - mean), axis=-1, keepdims=True)
    hf = (hf - mean) * lax.rsqrt(var + NORM_EPS) * ln_g.astype(F32) + ln_b.astype(F32)
    return jax.nn.silu(hf)


def even_mixer(h_lat, h_ctx, w_in, gate_w, gate_b, gla_norm_g, conv_w, conv_b, conv_ln_g, conv_ln_b, w_out, emit_ctx):
    bsz = h_lat.shape[0]
    zeros = jnp.zeros((bsz, GLA_HEADS, GLA_DK, GLA_DV), F32)
    p_ctx = split_cols(h_ctx @ w_in, EVEN_SIZES)
    p_lat = split_cols(h_lat @ w_in, EVEN_SIZES)
    o_ctx, s_f, s_b = gla_bidir(p_ctx[0], p_ctx[1], p_ctx[2], p_ctx[4], p_ctx[5], gate_w, gate_b, zeros, zeros, emit_ctx)
    o_lat, _, _ = gla_bidir(p_lat[0], p_lat[1], p_lat[2], p_lat[4], p_lat[5], gate_w, gate_b, s_f, s_b, True)

    def readout(p, o, on_grid):
        gla_out = head_rms_norm(o, gla_norm_g) * jax.nn.silu(p[3].astype(F32))
        conv_out = conformer_conv(p[6], p[7], conv_w, conv_b, conv_ln_g, conv_ln_b, on_grid)
        return jnp.concatenate([gla_out, conv_out], axis=-1).astype(w_out.dtype) @ w_out

    lat = readout(p_lat, o_lat, True)
    ctx = readout(p_ctx, o_ctx, False) if emit_ctx else None
    return lat, ctx


def rwkv_direction(feats, mu, w0, w2, a0, a2, k_k, k_a, r_k, s0, emit):
    bsz, t, _ = feats.shape
    feats = feats + (token_shift(feats) - feats) * mu
    r, k, v, wl, al = split_cols(feats, RWKV_SHIFT_SIZES)
    w = -jax.nn.softplus(-(w0 + jnp.tanh(wl) @ w2)) - 0.5
    decay = jnp.exp(-jnp.exp(w))
    a = jax.nn.sigmoid(a0 + al @ a2)

    def hd(z):
        return z.reshape(bsz, t, RWKV_HEADS, RWKV_HEAD)

    r, k, v, decay, a = hd(r), hd(k), hd(v), hd(decay), hd(a)
    kk = k * k_k.reshape(RWKV_HEADS, RWKV_HEAD)
    kk = kk / jnp.maximum(jnp.sqrt(jnp.sum(kk * kk, axis=-1, keepdims=True)), 1e-12)
    k = k * (1.0 + (a - 1.0) * k_a.reshape(RWKV_HEADS, RWKV_HEAD))

    def step(s, xs):
        r_t, w_t, k_t, v_t, kk_t, a_t = xs
        s = (s * w_t[:, :, None, :]
             + jnp.einsum('bhvk,bhk->bhv', s, -kk_t)[..., None] * (kk_t * a_t)[:, :, None, :]
             + v_t[..., None] * k_t[:, :, None, :])
        y_t = jnp.einsum('bhvk,bhk->bhv', s, r_t) if emit else None
        return s, y_t

    xs = tuple(jnp.moveaxis(z, 1, 0) for z in (r, decay, k, v, kk, a))
    s_fin, y = lax.scan(step, s0, xs)
    if not emit:
        return None, None, s_fin
    bonus = jnp.sum(r * k * r_k.reshape(RWKV_HEADS, RWKV_HEAD), axis=-1, keepdims=True) * v
    return jnp.moveaxis(y, 0, 1), bonus, s_fin


def rwkv_bidir(parts, mu, w0, w2, a0, a2, g2, k_k, k_a, r_k, ln_g, ln_b, s0_f, s0_b, emit):
    r, k, v, wl_f, wl_b, al_f, al_b, gl = [p.astype(F32) for p in parts]
    bsz, t, _ = r.shape
    y_f, bonus_f, s_f = rwkv_direction(jnp.concatenate([r, k, v, wl_f, al_f], axis=-1),
                                       mu[0], w0[0], w2[0], a0[0], a2[0], k_k, k_a, r_k, s0_f, emit)
    y_b, bonus_b, s_b = rwkv_direction(flip_t(jnp.concatenate([r, k, v, wl_b, al_b], axis=-1)),
                                       mu[1], w0[1], w2[1], a0[1], a2[1], k_k, k_a, r_k, s0_b, emit)
    if not emit:
        return None, s_f, s_b
    y = y_f + flip_t(y_b)
    mean = jnp.mean(y, axis=-1, keepdims=True)
    var = jnp.mean(jnp.square(y - mean), axis=-1, keepdims=True)
    y = ((y - mean) * lax.rsqrt(var + RWKV_GN_EPS) * ln_g.astype(F32).reshape(RWKV_HEADS, RWKV_HEAD)
         + ln_b.astype(F32).reshape(RWKV_HEADS, RWKV_HEAD))
    y = (y + bonus_f + flip_t(bonus_b)).reshape(bsz, t, RWKV_WIDTH)
    return y * (jax.nn.sigmoid(gl) @ g2.astype(F32)), s_f, s_b


def _linear_combine(e1, e2):
    a1, b1 = e1
    a2, b2 = e2
    return a1 * a2, a2 * b1 + b2


def s5_scan(ug, lam_re, lam_im, log_dt, b_cplx, s0):
    lam = lax.complex(jnp.minimum(lam_re.astype(F32), S5_MAX_RE), lam_im.astype(F32))
    lam_bar = jnp.exp(lam * jnp.exp(log_dt.astype(F32))[:, None])
    b_bar = ((lam_bar - 1.0) / lam)[..., None] * b_cplx
    bu = jnp.einsum('gpc,btgc->btgp', b_bar, ug)
    bu = bu.at[:, 0].add(lam_bar * s0)
    decay = jnp.broadcast_to(lam_bar, bu.shape)
    _, states = lax.associative_scan(_linear_combine, (decay, bu), axis=1)
    return states


def s5_bidir(u, lam_re, lam_im, log_dt, b_re, b_im, c_re, c_im, d_skip, glu_w, glu_b, s0_f, s0_b, emit):
    bsz, t, _ = u.shape
    ug = u.astype(F32).reshape(bsz, t, S5_GROUPS, S5_GROUP)
    b_c = lax.complex(b_re.astype(F32), b_im.astype(F32))
    st_f = s5_scan(ug, lam_re[0], lam_im[0], log_dt[0], b_c, s0_f)
    st_b = flip_t(s5_scan(flip_t(ug), lam_re[1], lam_im[1], log_dt[1], b_c, s0_b))
    s_f, s_b = st_f[:, -1], st_b[:, 0]
    if not emit:
        return None, s_f, s_b
    c_c = lax.complex(c_re.astype(F32), c_im.astype(F32))
    y = (jnp.real(jnp.einsum('gcp,btgp->btgc', c_c, st_f + st_b))
         + d_skip.astype(F32).reshape(S5_GROUPS, S5_GROUP) * ug)
    y = jax.nn.gelu(y.reshape(bsz, t, S5_WIDTH))
    return y * jax.nn.sigmoid(y @ glu_w.astype(F32) + glu_b.astype(F32)), s_f, s_b


def odd_mixer(h_lat, h_ctx, w_in, mu, w0, w2, a0, a2, g2, k_k, k_a, r_k, ln_g, ln_b,
              lam_re, lam_im, log_dt, b_re, b_im, c_re, c_im, d_skip, glu_w, glu_b, w_out, emit_ctx):
    bsz = h_lat.shape[0]
    p_ctx = split_cols(h_ctx @ w_in, ODD_SIZES)
    p_lat = split_cols(h_lat @ w_in, ODD_SIZES)
    zeros_r = jnp.zeros((bsz, RWKV_HEADS, RWKV_HEAD, RWKV_HEAD), F32)
    y_ctx, rs_f, rs_b = rwkv_bidir(p_ctx[:8], mu, w0, w2, a0, a2, g2, k_k, k_a, r_k, ln_g, ln_b, zeros_r, zeros_r, emit_ctx)
    y_lat, _, _ = rwkv_bidir(p_lat[:8], mu, w0, w2, a0, a2, g2, k_k, k_a, r_k, ln_g, ln_b, rs_f, rs_b, True)
    zeros_s = jnp.zeros((bsz, S5_GROUPS, S5_STATE), jnp.complex64)
    u_ctx, ss_f, ss_b = s5_bidir(p_ctx[8], lam_re, lam_im, log_dt, b_re, b_im, c_re, c_im, d_skip, glu_w, glu_b,
                                 zeros_s, zeros_s, emit_ctx)
    u_lat, _, _ = s5_bidir(to_col_major(p_lat[8]), lam_re, lam_im, log_dt, b_re, b_im, c_re, c_im, d_skip,
                           glu_w, glu_b, ss_f, ss_b, True)
    u_lat = from_col_major(u_lat)
    lat = jnp.concatenate([y_lat, u_lat], axis=-1).astype(w_out.dtype) @ w_out
    ctx = jnp.concatenate([y_ctx, u_ctx], axis=-1).astype(w_out.dtype) @ w_out if emit_ctx else None
    return lat, ctx


def moe_ffn(tokens, router_w, router_b, w1, b1, w2, b2):
    logits = (tokens @ router_w + router_b).astype(F32)
    top_v, top_i = lax.top_k(logits, TOP_K)
    weights = jax.nn.softmax(top_v, axis=-1)
    gate = jnp.einsum('mk,mke->me', weights, jax.nn.one_hot(top_i, N_EXPERTS, dtype=F32))
    out = jnp.zeros(tokens.shape, F32)
    for e in range(N_EXPERTS):
        h = tokens @ w1[e] + b1[e]
        x_glu = jnp.minimum(h[:, 0::2], SWIGLU_LIMIT)
        x_lin = jnp.clip(h[:, 1::2], -SWIGLU_LIMIT, SWIGLU_LIMIT)
        act = x_glu * jax.nn.sigmoid(SWIGLU_ALPHA * x_glu) * (x_lin + 1.0)
        out = out + gate[:, e:e + 1] * (act @ w2[e] + b2[e])
    return out.astype(tokens.dtype)


def setup_inputs(seed: int = 0) -> dict:
    key = jax.random.key(seed)
    keys = iter(jax.random.split(key, 64))

    def normal(shape, scale):
        return jax.random.normal(next(keys), shape, F32) * scale

    def uniform(shape, lo, hi):
        return jax.random.uniform(next(keys), shape, F32, lo, hi)

    d = D_MODEL
    ne, no = N_EVEN, N_ODD
    s5_im_init = jnp.pi * jnp.arange(S5_STATE, dtype=F32)
    return {
        'x': normal((BATCH, SEQ, d), 1.0),
        'c': normal((BATCH, d), 1.0),
        'ctx': normal((BATCH, CTX_LEN, d), 1.0),
        'c_ctx': normal((d,), 1.0),
        'ada_w': normal((DEPTH, d, N_MOD * d), 0.5 * d ** -0.5),
        'ada_b': normal((DEPTH, N_MOD * d), 0.02),
        'norm1_g': 1.0 + normal((DEPTH, d), 0.05),
        'norm2_g': 1.0 + normal((DEPTH, d), 0.05),
        'even_w_in': normal((ne, d, EVEN_IN), d ** -0.5),
        'gla_gate_w': normal((ne, 2, GLA_GATE_RANK, GLA_K_WIDTH), GLA_GATE_RANK ** -0.5),
        'gla_gate_b': normal((ne, 2, GLA_K_WIDTH), 0.1),
        'gla_norm_g': 1.0 + normal((ne, GLA_DV), 0.05),
        'conv_w': normal((ne, CONV_WIDTH, CONV_CH), CONV_WIDTH ** -0.5),
        'conv_b': normal((ne, CONV_CH), 0.02),
        'conv_ln_g': 1.0 + normal((ne, CONV_CH), 0.05),
        'conv_ln_b': normal((ne, CONV_CH), 0.02),
        'even_w_out': normal((ne, d, d), d ** -0.5),
        'odd_w_in': normal((no, d, ODD_IN), d ** -0.5),
        'rwkv_mu': uniform((no, 2, RWKV_SHIFT_WIDTH), 0.0, 1.0),
        'rwkv_w0': uniform((no, 2, RWKV_WIDTH), -6.0, -1.0),
        'rwkv_w2': normal((no, 2, RWKV_DECAY_RANK, RWKV_WIDTH), 0.5 * RWKV_DECAY_RANK ** -0.5),
        'rwkv_a0': normal((no, 2, RWKV_WIDTH), 0.1),
        'rwkv_a2': normal((no, 2, RWKV_ICLR_RANK, RWKV_WIDTH), 0.5 * RWKV_ICLR_RANK ** -0.5),
        'rwkv_g2': normal((no, RWKV_GATE_RANK, RWKV_WIDTH), RWKV_GATE_RANK ** -0.5),
        'rwkv_k_k': 0.85 + normal((no, RWKV_WIDTH), 0.05),
        'rwkv_k_a': 1.0 + normal((no, RWKV_WIDTH), 0.05),
        'rwkv_r_k': normal((no, RWKV_WIDTH), 0.05),
        'rwkv_ln_g': 1.0 + normal((no, RWKV_WIDTH), 0.05),
        'rwkv_ln_b': normal((no, RWKV_WIDTH), 0.02),
        's5_lam_re': -0.5 + normal((no, 2, S5_GROUPS, S5_STATE), 0.01),
        's5_lam_im': s5_im_init + normal((no, 2, S5_GROUPS, S5_STATE), 0.01),
        's5_log_dt': uniform((no, 2, S5_GROUPS), math.log(1e-3), math.log(1e-1)),
        's5_b_re': normal((no, S5_GROUPS, S5_STATE, S5_GROUP), (2 * S5_GROUP) ** -0.5),
        's5_b_im': normal((no, S5_GROUPS, S5_STATE, S5_GROUP), (2 * S5_GROUP) ** -0.5),
        's5_c_re': normal((no, S5_GROUPS, S5_GROUP, S5_STATE), S5_STATE ** -0.5),
        's5_c_im': normal((no, S5_GROUPS, S5_GROUP, S5_STATE), S5_STATE ** -0.5),
        's5_d': normal((no, S5_WIDTH), 1.0),
        's5_glu_w': normal((no, S5_WIDTH, S5_WIDTH), S5_WIDTH ** -0.5),
        's5_glu_b': normal((no, S5_WIDTH), 0.02),
        'odd_w_out': normal((no, d, d), d ** -0.5),
        'router_w': normal((DEPTH, d, N_EXPERTS), d ** -0.5),
        'router_b': normal((DEPTH, N_EXPERTS), 0.01),
        'moe_w1': normal((DEPTH, N_EXPERTS, d, 2 * D_EXPERT), d ** -0.5),
        'moe_b1': normal((DEPTH, N_EXPERTS, 2 * D_EXPERT), 0.02),
        'moe_w2': normal((DEPTH, N_EXPERTS, D_EXPERT, d), D_EXPERT ** -0.5),
        'moe_b2': normal((DEPTH, N_EXPERTS, d), 0.02),
        'final_g': 1.0 + normal((d,), 0.05),
    }


def reference(x, c, ctx, c_ctx, ada_w, ada_b, norm1_g, norm2_g,
              even_w_in, gla_gate_w, gla_gate_b, gla_norm_g, conv_w, conv_b, conv_ln_g, conv_ln_b, even_w_out,
              odd_w_in, rwkv_mu, rwkv_w0, rwkv_w2, rwkv_a0, rwkv_a2, rwkv_g2, rwkv_k_k, rwkv_k_a, rwkv_r_k,
              rwkv_ln_g, rwkv_ln_b, s5_lam_re, s5_lam_im, s5_log_dt, s5_b_re, s5_b_im, s5_c_re, s5_c_im,
              s5_d, s5_glu_w, s5_glu_b, odd_w_out,
              router_w, router_b, moe_w1, moe_b1, moe_w2, moe_b2, final_g):
    x_lat, x_ctx = x, ctx
    for layer in range(DEPTH):
        emit_ctx = layer < DEPTH - 1
        mod_lat = jax.nn.silu(c) @ ada_w[layer] + ada_b[layer]
        mod_ctx = jax.nn.silu(c_ctx) @ ada_w[layer] + ada_b[layer]
        sh1, sc1, g1, sh2, sc2, g2 = jnp.split(mod_lat[:, None, :], N_MOD, axis=-1)
        csh1, csc1, cg1, csh2, csc2, cg2 = jnp.split(mod_ctx, N_MOD, axis=-1)
        h_lat = modulate(rms_norm(x_lat, norm1_g[layer]), sh1, sc1)
        h_ctx = modulate(rms_norm(x_ctx, norm1_g[layer]), csh1, csc1)
        i = layer // 2
        if layer % 2 == 0:
            m_lat, m_ctx = even_mixer(h_lat, h_ctx, even_w_in[i], gla_gate_w[i], gla_gate_b[i], gla_norm_g[i],
                                      conv_w[i], conv_b[i], conv_ln_g[i], conv_ln_b[i], even_w_out[i], emit_ctx)
        else:
            m_lat, m_ctx = odd_mixer(h_lat, h_ctx, odd_w_in[i], rwkv_mu[i], rwkv_w0[i], rwkv_w2[i], rwkv_a0[i],
                                     rwkv_a2[i], rwkv_g2[i], rwkv_k_k[i], rwkv_k_a[i], rwkv_r_k[i], rwkv_ln_g[i],
                                     rwkv_ln_b[i], s5_lam_re[i], s5_lam_im[i], s5_log_dt[i], s5_b_re[i], s5_b_im[i],
                                     s5_c_re[i], s5_c_im[i], s5_d[i], s5_glu_w[i], s5_glu_b[i], odd_w_out[i], emit_ctx)
        x_lat = x_lat + g1 * m_lat
        if emit_ctx:
            x_ctx = x_ctx + cg1 * m_ctx
        f_lat = modulate(rms_norm(x_lat, norm2_g[layer]), sh2, sc2)
        bsz, t, d = f_lat.shape
        tokens = f_lat.reshape(bsz * t, d)
        if emit_ctx:
            f_ctx = modulate(rms_norm(x_ctx, norm2_g[layer]), csh2, csc2)
            tokens = jnp.concatenate([tokens, f_ctx.reshape(-1, d)], axis=0)
        y = moe_ffn(tokens, router_w[layer], router_b[layer], moe_w1[layer], moe_b1[layer],
                    moe_w2[layer], moe_b2[layer])
        x_lat = x_lat + g2 * y[:bsz * t].reshape(bsz, t, d)
        if emit_ctx:
            x_ctx = x_ctx + cg2 * y[bsz * t:].reshape(x_ctx.shape)
    return rms_norm(x_lat, final_g)
```

```python
import functools
import math

import jax
import jax.numpy as jnp
import numpy as np
from jax import lax
from jax.experimental import pallas as pl
from jax.experimental.pallas import tpu as pltpu

F32 = jnp.float32
BF16 = jnp.bfloat16

NORM_EPS = 1e-6
GRID_W = 64
N_MOD = 6
LANES = 128
SUBLANES = 8
VMEM_BYTES = 64 << 20

GLA_HEADS = 4
GLA_DK = 128
GLA_DV = 256
GLA_RANK = 16
GLA_NORMALIZER = 16.0
GLA_CHUNK = 64
CONV_WIDTH = 31
CONV_PAD = 16
TOKEN_BLOCK = 256

RWKV_HEAD = 64
RWKV_GN_EPS = 64e-5
S5_GROUP = 16
S5_STATE = 64
S5_MAX_RE = -1e-4

N_EXPERTS = 32
TOP_K = 4
SWIGLU_ALPHA = 1.702
SWIGLU_LIMIT = 7.0


def _params(semantics, vmem_mb):
    return pltpu.CompilerParams(dimension_semantics=semantics, vmem_limit_bytes=vmem_mb << 20)


def _bdot(a, b):
    return jnp.dot(a.astype(BF16), b.astype(BF16), preferred_element_type=F32)


def _split2(x):
    hi = x.astype(BF16)
    lo = (x - hi.astype(F32)).astype(BF16)
    return hi, lo


def _dot3(a, b):
    ah, al = _split2(a)
    bh, bl = _split2(b)
    d = functools.partial(jnp.dot, preferred_element_type=F32)
    return d(ah, bh) + d(al, bh) + d(ah, bl)


def _dot_sel(sel, b):
    b1 = b.astype(BF16)
    r1 = b - b1.astype(F32)
    b2 = r1.astype(BF16)
    b3 = (r1 - b2.astype(F32)).astype(BF16)
    d = functools.partial(jnp.dot, preferred_element_type=F32)
    return d(sel, b1) + d(sel, b2) + d(sel, b3)


def _dot_rsel(a, sel):
    a1 = a.astype(BF16)
    r1 = a - a1.astype(F32)
    a2 = r1.astype(BF16)
    a3 = (r1 - a2.astype(F32)).astype(BF16)
    d = functools.partial(jnp.dot, preferred_element_type=F32)
    return d(a1, sel) + d(a2, sel) + d(a3, sel)


def _sigmoid(x):
    return 1.0 / (1.0 + jnp.exp(-x))


def _silu(x):
    return x * _sigmoid(x)


def _log_sigmoid(x):
    return jnp.minimum(x, 0.0) - jnp.log(1.0 + jnp.exp(-jnp.abs(x)))


def _softplus(x):
    return jnp.maximum(x, 0.0) + jnp.log(1.0 + jnp.exp(-jnp.abs(x)))


class Dims:
    def __init__(self, batch, seq, ctx_len):
        self.B, self.T, self.C = batch, seq, ctx_len
        self.m_lat = batch * seq
        self.m_ctx = batch * ctx_len
        self.m_all = self.m_lat + self.m_ctx
        assert ctx_len == TOKEN_BLOCK and seq % TOKEN_BLOCK == 0 and seq % GRID_W == 0
        self.nlb = seq // TOKEN_BLOCK
        self.tm = math.gcd(math.gcd(seq, self.m_ctx), 1024)

    def sample_of_tile(self, i, tm):
        return jnp.where(i * tm < self.m_lat, (i * tm) // self.T, self.B)


def _adaln_kernel(c_ref, w_ref, b_ref, o_ref):
    o_ref[...] = _bdot(_silu(c_ref[...]), w_ref[...]) + b_ref[...]


def adaln_table(cvec, ada_w, ada_b):
    depth, d, n = ada_w.shape
    tn = 1024
    return pl.pallas_call(
        _adaln_kernel,
        grid=(depth, n // tn),
        in_specs=[
            pl.BlockSpec((SUBLANES, d), lambda l, j: (0, 0)),
            pl.BlockSpec((None, d, tn), lambda l, j: (l, 0, j)),
            pl.BlockSpec((None, 1, tn), lambda l, j: (l, 0, j)),
        ],
        out_specs=pl.BlockSpec((None, SUBLANES, tn), lambda l, j: (l, 0, j)),
        out_shape=jax.ShapeDtypeStruct((depth, SUBLANES, n), F32),
        compiler_params=_params(("parallel", "parallel"), 40),
    )(cvec, ada_w, ada_b.reshape(depth, 1, n))


def _normmod(x, g, shift, scale):
    y = x * lax.rsqrt(jnp.mean(x * x, axis=-1, keepdims=True) + NORM_EPS) * g
    return y * (1.0 + scale) + shift


def _normmod_mm_kernel(x_ref, g_ref, sh_ref, sc_ref, w_ref, o_ref, h_ref):
    @pl.when(pl.program_id(1) == 0)
    def _():
        h_ref[...] = _normmod(x_ref[...], g_ref[...], sh_ref[...], sc_ref[...]).astype(BF16)

    o_ref[...] = jnp.dot(h_ref[...], w_ref[...].astype(BF16), preferred_element_type=F32)


def normmod_matmul(dims, x_all, g, shift, scale, w, tn):
    m, d = x_all.shape
    n = w.shape[1]
    tm = dims.tm
    samp = lambda i, j: (dims.sample_of_tile(i, tm), 0, 0)
    return pl.pallas_call(
        _normmod_mm_kernel,
        grid=(m // tm, n // tn),
        in_specs=[
            pl.BlockSpec((tm, d), lambda i, j: (i, 0)),
            pl.BlockSpec((1, d), lambda i, j: (0, 0)),
            pl.BlockSpec((None, 1, d), samp),
            pl.BlockSpec((None, 1, d), samp),
            pl.BlockSpec((d, tn), lambda i, j: (0, j)),
        ],
        out_specs=pl.BlockSpec((tm, tn), lambda i, j: (i, j)),
        out_shape=jax.ShapeDtypeStruct((m, n), F32),
        scratch_shapes=[pltpu.VMEM((tm, d), BF16)],
        compiler_params=_params(("parallel", "arbitrary"), 56),
    )(x_all, g.reshape(1, d), shift, scale, w)


EVEN_Q, EVEN_K, EVEN_V, EVEN_OG, EVEN_CA, EVEN_CB, EVEN_LRF, EVEN_LRB = 0, 512, 1024, 2048, 3072, 4096, 5120, 5248
EVEN_COLS = 5376


def _gla_chunk(q, k, v, lr, gw, gb, st_ref, forward):
    length = q.shape[0]
    row = lax.broadcasted_iota(jnp.int32, (length, length), 0)
    col = lax.broadcasted_iota(jnp.int32, (length, length), 1)
    tri = (col <= row) if forward else (col >= row)
    g = _log_sigmoid(_dot3(lr, gw) + gb) * (1.0 / GLA_NORMALIZER)
    cum = _dot_sel(tri.astype(BF16), g)
    tot = cum[length - 1:length, :] if forward else cum[0:1, :]
    q_dec = (q * (GLA_DK ** -0.5) * jnp.exp(cum)).astype(BF16)
    k_inv = (k * jnp.exp(-cum)).astype(BF16)
    k_st = (k * jnp.exp(tot - cum)).astype(BF16)
    scores = lax.dot_general(q_dec, k_inv, (((1,), (1,)), ((), ())), preferred_element_type=F32)
    scores = jnp.where(tri, scores, 0.0)
    st = st_ref[...]
    o = _bdot(scores, v) + lax.dot_general(q_dec, st.astype(BF16), (((1,), (1,)), ((), ())),
                                           preferred_element_type=F32)
    st_ref[...] = st * jnp.exp(tot) + lax.dot_general(
        v.astype(BF16), k_st, (((0,), (0,)), ((), ())), preferred_element_type=F32)
    return o


def _gla_kernel(qf, kf, vf, lf, qb, kb, vb, lb, gwf, gbf, gwb, gbb, of, ob, sf, sb):
    @pl.when(pl.program_id(2) == 0)
    def _():
        sf[...] = jnp.zeros_like(sf)
        sb[...] = jnp.zeros_like(sb)

    nchunk = TOKEN_BLOCK // GLA_CHUNK
    for c in range(nchunk):
        r = pl.ds(c * GLA_CHUNK, GLA_CHUNK)
        of[r, :] = _gla_chunk(qf[r, :], kf[r, :], vf[r, :], lf[r, :], gwf[...], gbf[...], sf, True)
        r = pl.ds((nchunk - 1 - c) * GLA_CHUNK, GLA_CHUNK)
        ob[r, :] = _gla_chunk(qb[r, :], kb[r, :], vb[r, :], lb[r, :], gwb[...], gbb[...], sb, False)


def gla_scan(dims, proj, gate_w_pad, gate_b):
    B, nlb = dims.B, dims.nlb
    blk = TOKEN_BLOCK

    def fblk(b, s):
        return jnp.where(s == 0, B * nlb + b, b * nlb + s - 1)

    def bblk(b, s):
        return jnp.where(s == 0, B * nlb + b, b * nlb + nlb - s)

    def col(blk_fn, width, start):
        return pl.BlockSpec((blk, width), lambda b, h, s: (blk_fn(b, s), start // width + h))

    def lr(blk_fn, start):
        return pl.BlockSpec((blk, LANES), lambda b, h, s: (blk_fn(b, s), start // LANES))

    def gw(d):
        return pl.BlockSpec((None, LANES, GLA_DK), lambda b, h, s: (d, 0, h))

    def gb(d):
        return pl.BlockSpec((None, 1, GLA_DK), lambda b, h, s: (d, 0, h))

    out = jax.ShapeDtypeStruct((dims.m_all, GLA_HEADS * GLA_DV), F32)
    return pl.pallas_call(
        _gla_kernel,
        grid=(B, GLA_HEADS, nlb + 1),
        in_specs=[
            col(fblk, GLA_DK, EVEN_Q), col(fblk, GLA_DK, EVEN_K), col(fblk, GLA_DV, EVEN_V), lr(fblk, EVEN_LRF),
            col(bblk, GLA_DK, EVEN_Q), col(bblk, GLA_DK, EVEN_K), col(bblk, GLA_DV, EVEN_V), lr(bblk, EVEN_LRB),
            gw(0), gb(0), gw(1), gb(1),
        ],
        out_specs=[
            pl.BlockSpec((blk, GLA_DV), lambda b, h, s: (fblk(b, s), h)),
            pl.BlockSpec((blk, GLA_DV), lambda b, h, s: (bblk(b, s), h)),
        ],
        out_shape=[out, out],
        scratch_shapes=[pltpu.VMEM((GLA_DV, GLA_DK), F32), pltpu.VMEM((GLA_DV, GLA_DK), F32)],
        compiler_params=_params(("parallel", "parallel", "arbitrary"), 32),
    )(proj, proj, proj, proj, proj, proj, proj, proj, gate_w_pad, gate_b, gate_w_pad, gate_b)


def _conv_kernel(ca_ref, cb_ref, w_ref, b_ref, g_ref, beta_ref, o_ref, pad_ref, *, row_len):
    blk, ch = o_ref.shape
    seg = row_len + 2 * CONV_PAD
    nrows = blk // row_len
    h = ca_ref[...] * _sigmoid(cb_ref[...])
    for r in range(nrows):
        pad_ref[pl.ds(r * seg, CONV_PAD), :] = jnp.zeros((CONV_PAD, ch), F32)
        pad_ref[pl.ds(r * seg + CONV_PAD, row_len), :] = h[r * row_len:(r + 1) * row_len, :]
        pad_ref[pl.ds(r * seg + CONV_PAD + row_len, CONV_PAD), :] = jnp.zeros((CONV_PAD, ch), F32)
    half = CONV_WIDTH // 2
    sub = 64
    cw = 256
    for r in range(nrows):
        for m in range(row_len // sub):
            for c in range(ch // cw):
                cs = pl.ds(c * cw, cw)
                acc = jnp.zeros((sub, cw), F32)
                base = r * seg + CONV_PAD + m * sub - half
                for j in range(CONV_WIDTH):
                    acc = acc + w_ref[pl.ds(j, 1), cs] * pad_ref[pl.ds(base + j, sub), cs]
                o_ref[pl.ds(r * row_len + m * sub, sub), cs] = acc
    hf = o_ref[...] + b_ref[...]
    mean = jnp.mean(hf, axis=-1, keepdims=True)
    cen = hf - mean
    var = jnp.mean(cen * cen, axis=-1, keepdims=True)
    o_ref[...] = _silu(cen * lax.rsqrt(var + NORM_EPS) * g_ref[...] + beta_ref[...])


def conformer_conv(dims, proj, conv_w, conv_b, ln_g, ln_b):
    ch = conv_w.shape[1]
    blk = TOKEN_BLOCK
    n_lat = dims.m_lat // blk

    def run(row_len, nblk, blk0):
        vec = lambda: pl.BlockSpec((1, ch), lambda i: (0, 0))
        return pl.pallas_call(
            functools.partial(_conv_kernel, row_len=row_len),
            grid=(nblk,),
            in_specs=[
                pl.BlockSpec((blk, ch), lambda i: (blk0 + i, EVEN_CA // ch)),
                pl.BlockSpec((blk, ch), lambda i: (blk0 + i, EVEN_CB // ch)),
                pl.BlockSpec((CONV_WIDTH, ch), lambda i: (0, 0)),
                vec(), vec(), vec(),
            ],
            out_specs=pl.BlockSpec((blk, ch), lambda i: (i, 0)),
            out_shape=jax.ShapeDtypeStruct((nblk * blk, ch), F32),
            scratch_shapes=[pltpu.VMEM((blk // row_len * (row_len + 2 * CONV_PAD), ch), F32)],
            compiler_params=_params(("parallel",), 32),
        )(proj, proj, conv_w, conv_b.reshape(1, ch), ln_g.reshape(1, ch), ln_b.reshape(1, ch))

    lat = run(GRID_W, n_lat, 0)
    ctx = run(dims.C, dims.m_ctx // blk, n_lat)
    return jnp.concatenate([lat, ctx], axis=0)


def _even_out_kernel(x_ref, of_ref, ob_ref, og_ref, cv_ref, ng_ref, gate_ref, w_ref, o_ref, a_ref):
    @pl.when(pl.program_id(1) == 0)
    def _():
        for h in range(GLA_HEADS):
            cs = pl.ds(h * GLA_DV, GLA_DV)
            o = of_ref[:, cs] + ob_ref[:, cs]
            y = o * lax.rsqrt(jnp.mean(o * o, axis=-1, keepdims=True) + NORM_EPS) * ng_ref[...]
            a_ref[:, cs] = (y * _silu(og_ref[:, cs])).astype(BF16)
        nv = GLA_HEADS * GLA_DV
        a_ref[:, pl.ds(nv, cv_ref.shape[1])] = cv_ref[...].astype(BF16)

    m = jnp.dot(a_ref[...], w_ref[...].astype(BF16), preferred_element_type=F32)
    o_ref[...] = x_ref[...] + gate_ref[...] * m


def even_out(dims, x_all, proj, o_f, o_b, conv_out, gla_norm_g, w_out, gate, tn=512):
    m, d = x_all.shape
    tm = dims.tm
    nv = GLA_HEADS * GLA_DV
    return pl.pallas_call(
        _even_out_kernel,
        grid=(m // tm, d // tn),
        in_specs=[
            pl.BlockSpec((tm, tn), lambda i, j: (i, j)),
            pl.BlockSpec((tm, nv), lambda i, j: (i, 0)),
            pl.BlockSpec((tm, nv), lambda i, j: (i, 0)),
            pl.BlockSpec((tm, nv), lambda i, j: (i, EVEN_OG // nv)),
            pl.BlockSpec((tm, d - nv), lambda i, j: (i, 0)),
            pl.BlockSpec((1, GLA_DV), lambda i, j: (0, 0)),
            pl.BlockSpec((None, 1, tn), lambda i, j: (dims.sample_of_tile(i, tm), 0, j)),
            pl.BlockSpec((d, tn), lambda i, j: (0, j)),
        ],
        out_specs=pl.BlockSpec((tm, tn), lambda i, j: (i, j)),
        out_shape=jax.ShapeDtypeStruct((m, d), F32),
        scratch_shapes=[pltpu.VMEM((tm, d), BF16)],
        compiler_params=_params(("parallel", "arbitrary"), 56),
    )(x_all, o_f, o_b, proj, conv_out, gla_norm_g.reshape(1, GLA_DV), gate, w_out)


def even_weights(w_in, gate_w):
    d = w_in.shape[0]
    kw, vw = GLA_HEADS * GLA_DK, GLA_HEADS * GLA_DV
    main = 2 * kw + 2 * vw
    lr_f = w_in[:, main:main + GLA_RANK]
    lr_b = w_in[:, main + GLA_RANK:main + 2 * GLA_RANK]
    zpad = jnp.zeros((d, LANES - GLA_RANK), F32)
    w_pad = jnp.concatenate([w_in[:, :main], w_in[:, main + 2 * GLA_RANK:], lr_f, zpad, lr_b, zpad], axis=1)
    assert w_pad.shape[1] == EVEN_COLS
    gw_pad = jnp.pad(gate_w, ((0, 0), (0, LANES - GLA_RANK), (0, 0)))
    return w_pad, gw_pad


def even_mixer_layer(dims, x_all, mod, norm_g, w_in, gate_w, gate_b, gla_norm_g, conv_w, conv_b, conv_ln_g,
                     conv_ln_b, w_out):
    w_pad, gw_pad = even_weights(w_in, gate_w)
    proj = normmod_matmul(dims, x_all, norm_g, mod[0], mod[1], w_pad, tn=768)
    o_f, o_b = gla_scan(dims, proj, gw_pad, gate_b.reshape(2, 1, -1))
    conv_out = conformer_conv(dims, proj, conv_w, conv_b, conv_ln_g, conv_ln_b)
    return even_out(dims, x_all, proj, o_f, o_b, conv_out, gla_norm_g, w_out, mod[2])


EXPERT_TILE = 512
UP_CHUNK = 1024
DOWN_CHUNK = 1024
ROW_TILE = 256
NEG_INF = float("-inf")


def _router_kernel(x_ref, g_ref, sh_ref, sc_ref, rw_ref, rb_ref, f_ref, ti_ref, tw_ref, pos_ref, cnt_ref, carry_ref):
    @pl.when(pl.program_id(0) == 0)
    def _():
        carry_ref[...] = jnp.zeros_like(carry_ref)

    f = _normmod(x_ref[...], g_ref[...], sh_ref[...], sc_ref[...])
    f_ref[...] = f
    tm = f.shape[0]
    work = _dot3(f, rw_ref[...]) + rb_ref[...]
    lane = lax.broadcasted_iota(jnp.int32, (tm, LANES), 1)
    vals, idxs = [], []
    for _ in range(TOP_K):
        m = jnp.max(work, axis=-1, keepdims=True)
        idx = jnp.min(jnp.where(work == m, lane, LANES), axis=-1, keepdims=True)
        vals.append(m)
        idxs.append(idx)
        work = jnp.where(lane == idx, NEG_INF, work)
    es = [jnp.exp(v - vals[0]) for v in vals]
    denom = es[0] + es[1] + es[2] + es[3]
    sel = (work == NEG_INF).astype(F32)
    row = lax.broadcasted_iota(jnp.int32, (tm, tm), 0)
    col = lax.broadcasted_iota(jnp.int32, (tm, tm), 1)
    before = jnp.dot((col < row).astype(BF16), sel.astype(BF16), preferred_element_type=F32)
    rank = before + carry_ref[...]
    ti = jnp.zeros((tm, LANES), jnp.int32)
    tw = jnp.zeros((tm, LANES), F32)
    pos = jnp.zeros((tm, LANES), jnp.int32)
    for k in range(TOP_K):
        pk = jnp.sum(jnp.where(lane == idxs[k], rank, 0.0), axis=-1, keepdims=True).astype(jnp.int32)
        ti = jnp.where(lane == k, idxs[k], ti)
        tw = jnp.where(lane == k, es[k] / denom, tw)
        pos = jnp.where(lane == k, pk, pos)
    ti_ref[...] = ti
    tw_ref[...] = tw
    pos_ref[...] = pos
    carry_ref[...] = carry_ref[...] + jnp.sum(sel, axis=0, keepdims=True)
    cnt_ref[...] = carry_ref[...]


def moe_router(dims, x_all, m_rows, g, shift, scale, router_w, router_b):
    d = x_all.shape[1]
    tm = min(dims.tm, 512)
    rw = jnp.pad(router_w, ((0, 0), (0, LANES - N_EXPERTS)))
    rb = jnp.concatenate([router_b, jnp.full((LANES - N_EXPERTS,), -1e30, F32)]).reshape(1, LANES)
    samp = lambda i: (dims.sample_of_tile(i, tm), 0, 0)
    lane_out = lambda dt: jax.ShapeDtypeStruct((m_rows, LANES), dt)
    lane_spec = lambda: pl.BlockSpec((tm, LANES), lambda i: (i, 0))
    return pl.pallas_call(
        _router_kernel,
        grid=(m_rows // tm,),
        in_specs=[
            pl.BlockSpec((tm, d), lambda i: (i, 0)),
            pl.BlockSpec((1, d), lambda i: (0, 0)),
            pl.BlockSpec((None, 1, d), samp),
            pl.BlockSpec((None, 1, d), samp),
            pl.BlockSpec((d, LANES), lambda i: (0, 0)),
            pl.BlockSpec((1, LANES), lambda i: (0, 0)),
        ],
        out_specs=[pl.BlockSpec((tm, d), lambda i: (i, 0)), lane_spec(), lane_spec(), lane_spec(),
                   pl.BlockSpec((1, LANES), lambda i: (0, 0))],
        out_shape=[jax.ShapeDtypeStruct((m_rows, d), F32), lane_out(jnp.int32), lane_out(F32), lane_out(jnp.int32),
                   jax.ShapeDtypeStruct((1, LANES), F32)],
        scratch_shapes=[pltpu.VMEM((1, LANES), F32)],
        compiler_params=_params(("arbitrary",), 40),
    )(x_all, g.reshape(1, d), shift, scale, rw, rb)


def _dispatch_kernel(slot_ref, f_hbm, xs_in, xs_hbm, sem):
    del xs_in
    base = pl.program_id(0) * ROW_TILE

    def copy(r, k):
        return pltpu.make_async_copy(f_hbm.at[pl.ds(base + r, 1)], xs_hbm.at[pl.ds(slot_ref[r * TOP_K + k], 1)], sem)

    def issue(r, carry):
        for k in range(TOP_K):
            copy(r, k).start()
        return carry

    def drain(r, carry):
        for k in range(TOP_K):
            copy(r, k).wait()
        return carry

    lax.fori_loop(0, ROW_TILE, issue, 0)
    lax.fori_loop(0, ROW_TILE, drain, 0)


def moe_dispatch(f, slot_flat, n_slots):
    m, d = f.shape
    xs0 = jnp.zeros((n_slots, d), F32)
    return pl.pallas_call(
        _dispatch_kernel,
        grid=(m // ROW_TILE,),
        in_specs=[
            pl.BlockSpec((ROW_TILE * TOP_K,), lambda i: (i,), memory_space=pltpu.SMEM),
            pl.BlockSpec(memory_space=pl.ANY),
            pl.BlockSpec(memory_space=pl.ANY),
        ],
        out_specs=pl.BlockSpec(memory_space=pl.ANY),
        out_shape=jax.ShapeDtypeStruct((n_slots, d), F32),
        scratch_shapes=[pltpu.SemaphoreType.DMA(())],
        input_output_aliases={2: 0},
        compiler_params=_params(("arbitrary",), 16),
    )(slot_flat, f, xs0)


def _expert_up_kernel(te_ref, tv_ref, x_ref, w_ref, b_ref, h_ref):
    del te_ref
    half = UP_CHUNK // 2

    @pl.when(tv_ref[pl.program_id(1)] != 0)
    def _():
        h = jnp.dot(x_ref[...].astype(BF16), w_ref[...].astype(BF16), preferred_element_type=F32) + b_ref[...]
        even = lax.broadcasted_iota(jnp.int32, (h.shape[0], half), 1) % 2 == 0

        def act(hh):
            glu = jnp.minimum(hh, SWIGLU_LIMIT)
            lin = jnp.clip(hh, -SWIGLU_LIMIT, SWIGLU_LIMIT) + 1.0
            return glu * _sigmoid(SWIGLU_ALPHA * glu) * pltpu.roll(lin, half - 1, axis=1)

        packed = jnp.where(even, act(h[:, :half]), pltpu.roll(act(h[:, half:]), 1, axis=1))
        h_ref[...] = packed.astype(BF16)

    @pl.when(tv_ref[pl.program_id(1)] == 0)
    def _():
        h_ref[...] = jnp.zeros_like(h_ref)


def _expert_down_kernel(te_ref, tv_ref, h_ref, w_ref, b_ref, y_ref):
    del te_ref

    @pl.when(tv_ref[pl.program_id(1)] != 0)
    def _():
        y_ref[...] = jnp.dot(h_ref[...], w_ref[...].astype(BF16), preferred_element_type=F32) + b_ref[...]

    @pl.when(tv_ref[pl.program_id(1)] == 0)
    def _():
        y_ref[...] = jnp.zeros_like(y_ref)


def moe_experts(xs, tile_expert, tile_valid, w1, b1, w2p, b2):
    n_slots, d = xs.shape
    n_exp, _, two_f = w1.shape
    f = two_f // 2
    nt = n_slots // EXPERT_TILE
    grid_spec = pltpu.PrefetchScalarGridSpec(
        num_scalar_prefetch=2,
        grid=(two_f // UP_CHUNK, nt),
        in_specs=[
            pl.BlockSpec((EXPERT_TILE, d), lambda j, t, te, tv: (t, 0)),
            pl.BlockSpec((None, d, UP_CHUNK), lambda j, t, te, tv: (te[t], 0, j)),
            pl.BlockSpec((None, 1, UP_CHUNK), lambda j, t, te, tv: (te[t], 0, j)),
        ],
        out_specs=pl.BlockSpec((EXPERT_TILE, UP_CHUNK // 2), lambda j, t, te, tv: (t, j)),
    )
    hidden = pl.pallas_call(
        _expert_up_kernel,
        grid_spec=grid_spec,
        out_shape=jax.ShapeDtypeStruct((n_slots, f), BF16),
        compiler_params=_params(("arbitrary", "arbitrary"), 48),
    )(tile_expert, tile_valid, xs, w1, b1.reshape(n_exp, 1, two_f))
    grid_spec = pltpu.PrefetchScalarGridSpec(
        num_scalar_prefetch=2,
        grid=(d // DOWN_CHUNK, nt),
        in_specs=[
            pl.BlockSpec((EXPERT_TILE, f), lambda j, t, te, tv: (t, 0)),
            pl.BlockSpec((None, f, DOWN_CHUNK), lambda j, t, te, tv: (te[t], 0, j)),
            pl.BlockSpec((None, 1, DOWN_CHUNK), lambda j, t, te, tv: (te[t], 0, j)),
        ],
        out_specs=pl.BlockSpec((EXPERT_TILE, DOWN_CHUNK), lambda j, t, te, tv: (t, j)),
    )
    return pl.pallas_call(
        _expert_down_kernel,
        grid_spec=grid_spec,
        out_shape=jax.ShapeDtypeStruct((n_slots, d), F32),
        compiler_params=_params(("arbitrary", "arbitrary"), 48),
    )(tile_expert, tile_valid, hidden, w2p, b2.reshape(n_exp, 1, d))


def _combine_kernel(slot_ref, x_ref, tw_ref, gate_ref, fg_ref, ys_hbm, o_ref, buf, sem, *, final_norm):
    def copy(r, k):
        return pltpu.make_async_copy(ys_hbm.at[pl.ds(slot_ref[r * TOP_K + k], 1)], buf.at[k, pl.ds(r, 1)], sem)

    def issue(r, carry):
        for k in range(TOP_K):
            copy(r, k).start()
        return carry

    def drain(r, carry):
        for k in range(TOP_K):
            copy(r, k).wait()
        return carry

    lax.fori_loop(0, ROW_TILE, issue, 0)
    lax.fori_loop(0, ROW_TILE, drain, 0)
    tw = tw_ref[...]
    y = tw[:, 0:1] * buf[0]
    for k in range(1, TOP_K):
        y = y + tw[:, k:k + 1] * buf[k]
    out = x_ref[...] + gate_ref[...] * y
    if final_norm:
        out = out * lax.rsqrt(jnp.mean(out * out, axis=-1, keepdims=True) + NORM_EPS) * fg_ref[...]
    o_ref[...] = out


def moe_combine(dims, x_all, m_rows, slot_flat, top_w, gate, ys, final_g):
    d = x_all.shape[1]
    tm = ROW_TILE
    fg = jnp.ones((1, d), F32) if final_g is None else final_g.reshape(1, d)
    return pl.pallas_call(
        functools.partial(_combine_kernel, final_norm=final_g is not None),
        grid=(m_rows // tm,),
        in_specs=[
            pl.BlockSpec((tm * TOP_K,), lambda i: (i,), memory_space=pltpu.SMEM),
            pl.BlockSpec((tm, d), lambda i: (i, 0)),
            pl.BlockSpec((tm, LANES), lambda i: (i, 0)),
            pl.BlockSpec((None, 1, d), lambda i: (dims.sample_of_tile(i, tm), 0, 0)),
            pl.BlockSpec((1, d), lambda i: (0, 0)),
            pl.BlockSpec(memory_space=pl.ANY),
        ],
        out_specs=pl.BlockSpec((tm, d), lambda i: (i, 0)),
        out_shape=jax.ShapeDtypeStruct((m_rows, d), F32),
        scratch_shapes=[pltpu.VMEM((TOP_K, tm, d), F32), pltpu.SemaphoreType.DMA(())],
        compiler_params=_params(("arbitrary",), 40),
    )(slot_flat, x_all, top_w, gate, fg, ys)


def moe_layer(dims, x_all, m_rows, mod, norm_g, router_w, router_b, w1, b1, w2, b2, final_g=None):
    n_exp, f, d = w2.shape
    f_tok, top_i, top_w, pos, cnt = moe_router(dims, x_all, m_rows, norm_g, mod[3], mod[4], router_w, router_b)
    counts = cnt[0, :n_exp].astype(jnp.int32)
    tiles_e = (counts + EXPERT_TILE - 1) // EXPERT_TILE
    tile_end = jnp.cumsum(tiles_e)
    offset = (tile_end - tiles_e) * EXPERT_TILE
    slot = (offset[top_i[:, :TOP_K]] + pos[:, :TOP_K]).reshape(-1)
    nt = m_rows * TOP_K // EXPERT_TILE + n_exp
    tile_ids = jnp.arange(nt, dtype=jnp.int32)
    tile_expert = jnp.minimum(jnp.searchsorted(tile_end, tile_ids, side="right"), n_exp - 1).astype(jnp.int32)
    tile_valid = (tile_ids < tile_end[-1]).astype(jnp.int32)
    xs = moe_dispatch(f_tok, slot, nt * EXPERT_TILE)
    half = UP_CHUNK // 4
    w2p = w2.reshape(n_exp, f // (2 * half), 2, half, d).swapaxes(2, 3).reshape(n_exp, f, d)
    ys = moe_experts(xs, tile_expert, tile_valid, w1, b1, w2p, b2)
    return moe_combine(dims, x_all, m_rows, slot, top_w, mod[5], ys, final_g)


ODD_R, ODD_K, ODD_V, ODD_WL, ODD_AL, ODD_GL, ODD_S5 = 0, 1536, 3072, 4608, 4736, 4864, 4992
ODD_COLS = 5632
RWKV_WIDTH = 1536
RWKV_HEADS = RWKV_WIDTH // RWKV_HEAD
RWKV_RANK = 64
CHAIN_GROUP = 64
SCAN_STEPS = 8


def _head_sum_matrices():
    ch = np.arange(RWKV_WIDTH)[:, None] // RWKV_HEAD
    e = (ch == np.arange(LANES)[None, :]).astype(np.float32)
    return jnp.asarray(e, BF16), jnp.asarray(e.T, BF16)


def _rwkv_prep_kernel(r_ref, k_ref, v_ref, wl_ref, al_ref, rn_ref, kn_ref, vn_ref, wln_ref, aln_ref,
                      mur, muk, muv, muwl, mual, w0, w2, a0, a2, kk_w, ka_w, rk_w, e_ref, et_ref,
                      dec_o, kk_o, b_o, k_o, wr_o, v_o, bonus_o, br_o, kr_o, *, forward, nlb, n_lat_blocks):
    i = pl.program_id(0)
    tm = r_ref.shape[0]
    is_lat = i < n_lat_blocks
    if forward:
        edge = jnp.logical_or(jnp.logical_not(is_lat), i % nlb == 0)
    else:
        edge = jnp.logical_or(jnp.logical_not(is_lat), i % nlb == nlb - 1)
    keep = jnp.where(edge, 0.0, 1.0)

    def mix(x_ref, n_ref, mu_ref):
        x = x_ref[...]
        rows = lax.broadcasted_iota(jnp.int32, x.shape, 0)
        if forward:
            sh = jnp.where(rows == 0, n_ref[SUBLANES - 1:SUBLANES, :] * keep, pltpu.roll(x, 1, axis=0))
        else:
            sh = jnp.where(rows == tm - 1, n_ref[0:1, :] * keep, pltpu.roll(x, tm - 1, axis=0))
        return x + (sh - x) * mu_ref[...]

    e = e_ref[...]
    et = et_ref[...]
    r = mix(r_ref, rn_ref, mur)
    k = mix(k_ref, kn_ref, muk)
    v = mix(v_ref, vn_ref, muv)
    wl = mix(wl_ref, wln_ref, muwl)
    al = mix(al_ref, aln_ref, mual)
    w = -_softplus(-(w0[...] + _dot3(jnp.tanh(wl), w2[...]))) - 0.5
    decay = jnp.exp(-jnp.exp(w))
    a = _sigmoid(a0[...] + _dot3(al, a2[...]))
    kk = k * kk_w[...]
    inv = 1.0 / jnp.maximum(jnp.sqrt(_dot_rsel(kk * kk, e)), 1e-12)
    kk = kk * _dot_rsel(inv, et)
    k = k * (1.0 + (a - 1.0) * ka_w[...])
    b = kk * a
    dec_o[...] = decay
    kk_o[...] = kk
    b_o[...] = b
    k_o[...] = k
    wr_o[...] = decay * r
    v_o[...] = v
    br_o[...] = _dot_rsel(b * r, e)
    kr_o[...] = _dot_rsel(k * r, e)
    bonus_o[...] = _dot_rsel(_dot_rsel(r * k * rk_w[...], e), et) * v


def rwkv_prep(dims, proj, direction, mu, w0, w2, a0, a2, k_k, k_a, r_k):
    wd = RWKV_WIDTH
    tm = TOKEN_BLOCK
    nblk = dims.m_all // tm
    forward = direction == 0
    per8 = tm // SUBLANES
    last8 = dims.m_all // SUBLANES - 1

    def nb(i):
        return jnp.maximum(i * per8 - 1, 0) if forward else jnp.minimum((i + 1) * per8, last8)

    def main(width, start):
        return pl.BlockSpec((tm, width), lambda i: (i, start // width))

    def neigh(width, start):
        return pl.BlockSpec((SUBLANES, width), lambda i: (nb(i), start // width))

    vec = lambda width: pl.BlockSpec((1, width), lambda i: (0, 0))
    lo = direction * RWKV_RANK
    lane_pad = lambda x: jnp.zeros((LANES,) + x.shape[1:], F32).at[lo:lo + RWKV_RANK].set(x)
    mu_d = mu[direction]
    e, et = _head_sum_matrices()
    wide = jax.ShapeDtypeStruct((dims.m_all, wd), F32)
    lane = jax.ShapeDtypeStruct((dims.m_all, LANES), F32)
    return pl.pallas_call(
        functools.partial(_rwkv_prep_kernel, forward=forward, nlb=dims.nlb, n_lat_blocks=dims.m_lat // tm),
        grid=(nblk,),
        in_specs=[main(wd, ODD_R), main(wd, ODD_K), main(wd, ODD_V), main(LANES, ODD_WL), main(LANES, ODD_AL),
                  neigh(wd, ODD_R), neigh(wd, ODD_K), neigh(wd, ODD_V), neigh(LANES, ODD_WL), neigh(LANES, ODD_AL),
                  vec(wd), vec(wd), vec(wd), vec(LANES), vec(LANES),
                  vec(wd), pl.BlockSpec((LANES, wd), lambda i: (0, 0)),
                  vec(wd), pl.BlockSpec((LANES, wd), lambda i: (0, 0)),
                  vec(wd), vec(wd), vec(wd),
                  pl.BlockSpec((wd, LANES), lambda i: (0, 0)), pl.BlockSpec((LANES, wd), lambda i: (0, 0))],
        out_specs=[pl.BlockSpec((tm, wd), lambda i: (i, 0))] * 7 + [pl.BlockSpec((tm, LANES), lambda i: (i, 0))] * 2,
        out_shape=[wide] * 7 + [lane] * 2,
        compiler_params=_params(("parallel",), 56),
    )(proj, proj, proj, proj, proj, proj, proj, proj, proj, proj,
      mu_d[0:wd].reshape(1, wd), mu_d[wd:2 * wd].reshape(1, wd), mu_d[2 * wd:3 * wd].reshape(1, wd),
      lane_pad(mu_d[3 * wd:3 * wd + RWKV_RANK]).reshape(1, LANES),
      lane_pad(mu_d[3 * wd + RWKV_RANK:3 * wd + 2 * RWKV_RANK]).reshape(1, LANES),
      w0[direction].reshape(1, wd), lane_pad(w2[direction]), a0[direction].reshape(1, wd), lane_pad(a2[direction]),
      k_k.reshape(1, wd), k_a.reshape(1, wd), r_k.reshape(1, wd), e, et)


def _rwkv_scan_kernel(w_ref, kk_ref, b_ref, k_ref, wr_ref, v_ref, br_ref, kr_ref, y_ref, st_ref, bc_ref):
    @pl.when(pl.program_id(0) == 0)
    def _():
        st_ref[...] = jnp.zeros_like(st_ref)

    ngroups = st_ref.shape[0]
    nk = st_ref.shape[1]
    hv = st_ref.shape[2]

    def dup(x):
        return jnp.concatenate([x, x], axis=-1)

    def step(t, carry):
        for g in range(ngroups):
            lanes = pl.ds(g * CHAIN_GROUP, CHAIN_GROUP)
            for n, ref in enumerate((w_ref, kk_ref, b_ref, k_ref, wr_ref)):
                bc_ref[n] = dup(ref[t, :, lanes])
            vfull = v_ref[t, :, lanes]
            v2 = jnp.concatenate([vfull[:hv], vfull[hv:]], axis=-1)
            br = jnp.broadcast_to(dup(br_ref[t, :, lanes]), (hv, LANES))
            kr = jnp.broadcast_to(dup(kr_ref[t, :, lanes]), (hv, LANES))

            def row(n, k):
                return jnp.broadcast_to(bc_ref[n, pl.ds(k, 1), :], (hv, LANES))

            def reduce_k(k, acc):
                sa, z = acc
                s = st_ref[g, k]
                return sa - s * row(1, k), z + s * row(4, k)

            zero = jnp.zeros((hv, LANES), F32)
            sa, z = lax.fori_loop(0, nk, reduce_k, (zero, zero), unroll=8)
            y = z + sa * br + v2 * kr

            def update_k(k, c):
                st_ref[g, k] = st_ref[g, k] * row(0, k) + sa * row(2, k) + v2 * row(3, k)
                return c

            lax.fori_loop(0, nk, update_k, 0, unroll=8)
            y_ref[t, pl.ds(0, hv), lanes] = y[:, :CHAIN_GROUP]
            y_ref[t, pl.ds(hv, hv), lanes] = y[:, CHAIN_GROUP:]
        return carry

    lax.fori_loop(0, SCAN_STEPS, step, 0)


def rwkv_scan(w, kk, b, k, wr, v, br, kr):
    steps, n, chains = v.shape
    ngroups = chains // CHAIN_GROUP
    big = pl.BlockSpec((SCAN_STEPS, n, chains), lambda i: (i, 0, 0))
    small = pl.BlockSpec((SCAN_STEPS, 1, chains), lambda i: (i, 0, 0))
    return pl.pallas_call(
        _rwkv_scan_kernel,
        grid=(steps // SCAN_STEPS,),
        in_specs=[big] * 6 + [small] * 2,
        out_specs=big,
        out_shape=jax.ShapeDtypeStruct((steps, n, chains), F32),
        scratch_shapes=[pltpu.VMEM((ngroups, n, n // 2, LANES), F32), pltpu.VMEM((5, n, LANES), F32)],
        compiler_params=_params(("arbitrary",), 32),
    )(w, kk, b, k, wr, v, br, kr)


def _scan_order(dims, a, forward, col_major=False):
    width = a.shape[1]
    lat = a[:dims.m_lat].reshape(dims.B, dims.T, width)
    if col_major:
        lat = lat.reshape(dims.B, dims.T // GRID_W, GRID_W, width).transpose(0, 2, 1, 3).reshape(dims.B, dims.T, width)
    ctx = a[dims.m_lat:].reshape(dims.B, dims.C, width)
    if not forward:
        lat, ctx = lat[:, ::-1], ctx[:, ::-1]
    return jnp.concatenate([ctx, lat], axis=1)


def _to_chains(dims, a_f, a_b, per_head):
    s = jnp.stack([_scan_order(dims, a_f, True), _scan_order(dims, a_b, False)])
    steps = s.shape[2]
    s = s.reshape(2, dims.B, steps, RWKV_HEADS, per_head).transpose(2, 4, 0, 1, 3)
    s = s.reshape(steps, per_head, 2 * dims.B * RWKV_HEADS)
    pad = -s.shape[2] % CHAIN_GROUP
    return jnp.pad(s, ((0, 0), (0, 0), (0, pad)))


def _from_chains(dims, y):
    nch = 2 * dims.B * RWKV_HEADS
    y = y[dims.C:, :, :nch].reshape(dims.T, RWKV_HEAD, 2, dims.B, RWKV_HEADS).transpose(2, 3, 0, 4, 1)
    y = y.reshape(2, dims.B, dims.T, RWKV_WIDTH)
    return y[0].reshape(dims.m_lat, RWKV_WIDTH), y[1, :, ::-1].reshape(dims.m_lat, RWKV_WIDTH)


def rwkv_mixer(dims, proj, mu, w0, w2, a0, a2, k_k, k_a, r_k):
    pf = rwkv_prep(dims, proj, 0, mu, w0, w2, a0, a2, k_k, k_a, r_k)
    pb = rwkv_prep(dims, proj, 1, mu, w0, w2, a0, a2, k_k, k_a, r_k)
    wide = [_to_chains(dims, pf[n], pb[n], RWKV_HEAD) for n in range(6)]
    br = _to_chains(dims, pf[7][:, :RWKV_HEADS], pb[7][:, :RWKV_HEADS], 1)
    kr = _to_chains(dims, pf[8][:, :RWKV_HEADS], pb[8][:, :RWKV_HEADS], 1)
    y = rwkv_scan(*wide, br, kr)
    y_f, y_b = _from_chains(dims, y)
    return y_f, y_b, pf[6], pb[6]


S5_WIDTH = 512
S5_GROUPS = S5_WIDTH // S5_GROUP
S5_LANES = S5_GROUPS * S5_STATE
S5_BLOCKS = S5_WIDTH // LANES
S5_STEPS = 32


def _s5_kernel(u_ref, lr_ref, li_ref, wre_ref, wim_ref, cre_ref, cim_ref, y_ref, xr_ref, xi_ref, sr_ref, si_ref,
               *, n_fwd_rows):
    @pl.when(pl.program_id(0) == 0)
    def _():
        sr_ref[...] = jnp.zeros_like(sr_ref)
        si_ref[...] = jnp.zeros_like(si_ref)

    rows = S5_STEPS * SUBLANES
    seg = S5_LANES // S5_BLOCKS
    u = u_ref[...].reshape(rows, S5_WIDTH).astype(BF16)
    backward = lax.broadcasted_iota(jnp.int32, (rows, 1), 0) % SUBLANES >= n_fwd_rows
    for q in range(S5_BLOCKS):
        uq = u[:, q * LANES:(q + 1) * LANES]
        cols = pl.ds(q * seg, seg)
        dot = functools.partial(jnp.dot, preferred_element_type=F32)
        xr_ref[:, cols] = jnp.where(backward, dot(uq, wre_ref[1, q]), dot(uq, wre_ref[0, q]))
        xi_ref[:, cols] = jnp.where(backward, dot(uq, wim_ref[1, q]), dot(uq, wim_ref[0, q]))

    lam_r = lr_ref[...]
    lam_i = li_ref[...]

    def step(t, carry):
        sr, si = carry
        r = pl.ds(pl.multiple_of(t * SUBLANES, SUBLANES), SUBLANES)
        nr = lam_r * sr - lam_i * si + xr_ref[r, :]
        ni = lam_r * si + lam_i * sr + xi_ref[r, :]
        xr_ref[r, :] = nr
        xi_ref[r, :] = ni
        return nr, ni

    sr, si = lax.fori_loop(0, S5_STEPS, step, (sr_ref[...], si_ref[...]))
    sr_ref[...] = sr
    si_ref[...] = si
    for q in range(S5_BLOCKS):
        cols = pl.ds(q * seg, seg)
        yq = _bdot(xr_ref[:, cols], cre_ref[q]) - _bdot(xi_ref[:, cols], cim_ref[q])
        y_ref[:, :, pl.ds(q * LANES, LANES)] = yq.reshape(S5_STEPS, SUBLANES, LANES)


def _s5_maps(lam_re, lam_im, log_dt, b_re, b_im, c_re, c_im):
    lre = jnp.minimum(lam_re, S5_MAX_RE)
    dt = jnp.exp(log_dt)[..., None]
    mag = jnp.exp(lre * dt)
    lbr, lbi = mag * jnp.cos(lam_im * dt), mag * jnp.sin(lam_im * dt)
    den = lre * lre + lam_im * lam_im
    cr = ((lbr - 1.0) * lre + lbi * lam_im) / den
    ci = (lbi * lre - (lbr - 1.0) * lam_im) / den
    bbr = cr[..., None] * b_re[None] - ci[..., None] * b_im[None]
    bbi = cr[..., None] * b_im[None] + ci[..., None] * b_re[None]
    gpb = S5_GROUPS // S5_BLOCKS
    eye = jnp.eye(gpb, dtype=F32)

    def in_map(bb):
        bb = bb.reshape(2, S5_BLOCKS, gpb, S5_STATE, S5_GROUP)
        return jnp.einsum("dqgpc,gh->dqgchp", bb, eye).reshape(2, S5_BLOCKS, LANES, gpb * S5_STATE).astype(BF16)

    def out_map(cc):
        cc = cc.reshape(S5_BLOCKS, gpb, S5_GROUP, S5_STATE)
        return jnp.einsum("qgcp,gh->qgphc", cc, eye).reshape(S5_BLOCKS, gpb * S5_STATE, LANES).astype(BF16)

    return lbr, lbi, in_map(bbr), in_map(bbi), out_map(c_re), out_map(c_im)


def s5_mixer(dims, u_all, lam_re, lam_im, log_dt, b_re, b_im, c_re, c_im):
    assert 2 * dims.B <= SUBLANES
    lbr, lbi, wre, wim, cre, cim = _s5_maps(lam_re, lam_im, log_dt, b_re, b_im, c_re, c_im)
    rows_dir = jnp.minimum(jnp.arange(SUBLANES) // dims.B, 1)
    lam_r = lbr.reshape(2, S5_LANES)[rows_dir]
    lam_i = lbi.reshape(2, S5_LANES)[rows_dir]
    u = jnp.stack([_scan_order(dims, u_all, True, col_major=True), _scan_order(dims, u_all, False, col_major=True)])
    steps = u.shape[2]
    u = u.reshape(2 * dims.B, steps, S5_WIDTH).transpose(1, 0, 2)
    u = jnp.pad(u, ((0, 0), (0, SUBLANES - 2 * dims.B), (0, 0)))
    full = lambda shape: pl.BlockSpec(shape, lambda i: (0,) * len(shape))
    rows = S5_STEPS * SUBLANES
    y = pl.pallas_call(
        functools.partial(_s5_kernel, n_fwd_rows=dims.B),
        grid=(steps // S5_STEPS,),
        in_specs=[pl.BlockSpec((S5_STEPS, SUBLANES, S5_WIDTH), lambda i: (i, 0, 0)),
                  full((SUBLANES, S5_LANES)), full((SUBLANES, S5_LANES)),
                  full(wre.shape), full(wim.shape), full(cre.shape), full(cim.shape)],
        out_specs=pl.BlockSpec((S5_STEPS, SUBLANES, S5_WIDTH), lambda i: (i, 0, 0)),
        out_shape=jax.ShapeDtypeStruct((steps, SUBLANES, S5_WIDTH), F32),
        scratch_shapes=[pltpu.VMEM((rows, S5_LANES), F32), pltpu.VMEM((rows, S5_LANES), F32),
                        pltpu.VMEM((SUBLANES, S5_LANES), F32), pltpu.VMEM((SUBLANES, S5_LANES), F32)],
        compiler_params=_params(("arbitrary",), 32),
    )(u, lam_r, lam_i, wre, wim, cre, cim)

    def latent(yd, forward):
        yd = yd[dims.C:].transpose(1, 0, 2)
        if not forward:
            yd = yd[:, ::-1]
        yd = yd.reshape(dims.B, GRID_W, dims.T // GRID_W, S5_WIDTH).transpose(0, 2, 1, 3)
        return yd.reshape(dims.m_lat, S5_WIDTH)

    return latent(y[:, :dims.B], True), latent(y[:, dims.B:2 * dims.B], False)


def _gelu_tanh(x):
    return 0.5 * x * (1.0 + jnp.tanh(math.sqrt(2.0 / math.pi) * (x + 0.044715 * x * x * x)))


def _odd_out_kernel(x_ref, yf_ref, yb_ref, bf_ref, bb_ref, gl_ref, sf_ref, sb_ref, u_ref, lng, lnb, g2_ref, dsk,
                    glw, glb, e_ref, et_ref, gate_ref, w_ref, o_ref, a_ref):
    @pl.when(pl.program_id(1) == 0)
    def _():
        e = e_ref[...]
        et = et_ref[...]
        y = yf_ref[...] + yb_ref[...]
        mean = _dot_rsel(_dot_rsel(y, e) * (1.0 / RWKV_HEAD), et)
        cen = y - mean
        var = _dot_rsel(cen * cen, e) * (1.0 / RWKV_HEAD)
        yn = cen * _dot_rsel(lax.rsqrt(var + RWKV_GN_EPS), et) * lng[...] + lnb[...]
        yn = yn + bf_ref[...] + bb_ref[...]
        a_ref[:, pl.ds(0, RWKV_WIDTH)] = (yn * _dot3(_sigmoid(gl_ref[...]), g2_ref[...])).astype(BF16)
        ys = _gelu_tanh(sf_ref[...] + sb_ref[...] + dsk[...] * u_ref[...])
        a_ref[:, pl.ds(RWKV_WIDTH, S5_WIDTH)] = (ys * _sigmoid(_dot3(ys, glw[...]) + glb[...])).astype(BF16)

    m = jnp.dot(a_ref[...], w_ref[...].astype(BF16), preferred_element_type=F32)
    o_ref[...] = x_ref[...] + gate_ref[...] * m


def odd_out(dims, x_all, proj, u_all, y_f, y_b, bonus_f, bonus_b, s_f, s_b, ln_g, ln_b, g2, d_skip, glu_w, glu_b,
            w_out, gate, tn=512):
    d = x_all.shape[1]
    tm = TOKEN_BLOCK
    wd = RWKV_WIDTH
    e, et = _head_sum_matrices()
    rowblk = lambda width: pl.BlockSpec((tm, width), lambda i, j: (i, 0))
    vec = lambda width: pl.BlockSpec((1, width), lambda i, j: (0, 0))
    mat = lambda r, c: pl.BlockSpec((r, c), lambda i, j: (0, 0))
    return pl.pallas_call(
        _odd_out_kernel,
        grid=(dims.m_lat // tm, d // tn),
        in_specs=[pl.BlockSpec((tm, tn), lambda i, j: (i, j)),
                  rowblk(wd), rowblk(wd), rowblk(wd), rowblk(wd),
                  pl.BlockSpec((tm, LANES), lambda i, j: (i, ODD_GL // LANES)),
                  rowblk(S5_WIDTH), rowblk(S5_WIDTH), rowblk(S5_WIDTH),
                  vec(wd), vec(wd), mat(LANES, wd), vec(S5_WIDTH), mat(S5_WIDTH, S5_WIDTH), vec(S5_WIDTH),
                  mat(wd, LANES), mat(LANES, wd),
                  pl.BlockSpec((None, 1, tn), lambda i, j: (dims.sample_of_tile(i, tm), 0, j)),
                  pl.BlockSpec((d, tn), lambda i, j: (0, j))],
        out_specs=pl.BlockSpec((tm, tn), lambda i, j: (i, j)),
        out_shape=jax.ShapeDtypeStruct((dims.m_lat, d), F32),
        scratch_shapes=[pltpu.VMEM((tm, d), BF16)],
        compiler_params=_params(("parallel", "arbitrary"), 56),
    )(x_all, y_f, y_b, bonus_f, bonus_b, proj, s_f, s_b, u_all, ln_g.reshape(1, wd), ln_b.reshape(1, wd), g2,
      d_skip.reshape(1, S5_WIDTH), glu_w, glu_b.reshape(1, S5_WIDTH), e, et, gate, w_out)


def odd_mixer_layer(dims, x_all, mod, norm_g, w_in, mu, w0, w2, a0, a2, g2, k_k, k_a, r_k, ln_g, ln_b,
                    lam_re, lam_im, log_dt, b_re, b_im, c_re, c_im, d_skip, glu_w, glu_b, w_out):
    w_pad = jnp.pad(w_in, ((0, 0), (0, ODD_COLS - w_in.shape[1])))
    proj = normmod_matmul(dims, x_all, norm_g, mod[0], mod[1], w_pad, tn=512)
    y_f, y_b, bonus_f, bonus_b = rwkv_mixer(dims, proj, mu, w0, w2, a0, a2, k_k, k_a, r_k)
    u_all = proj[:, ODD_S5:ODD_S5 + S5_WIDTH]
    s_f, s_b = s5_mixer(dims, u_all, lam_re, lam_im, log_dt, b_re, b_im, c_re, c_im)
    return odd_out(dims, x_all, proj, u_all, y_f, y_b, bonus_f, bonus_b, s_f, s_b, ln_g, ln_b, g2, d_skip, glu_w,
                   glu_b, w_out, mod[2])


def kernel(x, c, ctx, c_ctx, ada_w, ada_b, norm1_g, norm2_g, even_w_in, gla_gate_w, gla_gate_b, gla_norm_g, conv_w, conv_b, conv_ln_g, conv_ln_b, even_w_out, odd_w_in, rwkv_mu, rwkv_w0, rwkv_w2, rwkv_a0, rwkv_a2, rwkv_g2, rwkv_k_k, rwkv_k_a, rwkv_r_k, rwkv_ln_g, rwkv_ln_b, s5_lam_re, s5_lam_im, s5_log_dt, s5_b_re, s5_b_im, s5_c_re, s5_c_im, s5_d, s5_glu_w, s5_glu_b, odd_w_out, router_w, router_b, moe_w1, moe_b1, moe_w2, moe_b2, final_g):
    B, T, D = x.shape
    C = ctx.shape[1]
    assert ada_w.shape[0] == 2, "one even layer followed by one (final) odd layer"
    dims = Dims(B, T, C)
    cvec = jnp.zeros((SUBLANES, D), F32).at[:B].set(c).at[B].set(c_ctx)
    tab = adaln_table(cvec, ada_w, ada_b)
    x_all = jnp.concatenate([x.reshape(-1, D), ctx.reshape(-1, D)], axis=0)
    mod = tab[0].reshape(SUBLANES, N_MOD, 1, D).transpose(1, 0, 2, 3)
    x_all = even_mixer_layer(dims, x_all, mod, norm1_g[0], even_w_in[0], gla_gate_w[0], gla_gate_b[0],
                             gla_norm_g[0], conv_w[0], conv_b[0], conv_ln_g[0], conv_ln_b[0], even_w_out[0])
    x_all = moe_layer(dims, x_all, dims.m_all, mod, norm2_g[0], router_w[0], router_b[0], moe_w1[0], moe_b1[0],
                      moe_w2[0], moe_b2[0])
    mod = tab[1].reshape(SUBLANES, N_MOD, 1, D).transpose(1, 0, 2, 3)
    x_lat = odd_mixer_layer(dims, x_all, mod, norm1_g[1], odd_w_in[0], rwkv_mu[0], rwkv_w0[0], rwkv_w2[0],
                            rwkv_a0[0], rwkv_a2[0], rwkv_g2[0], rwkv_k_k[0], rwkv_k_a[0], rwkv_r_k[0],
                            rwkv_ln_g[0], rwkv_ln_b[0], s5_lam_re[0], s5_lam_im[0], s5_log_dt[0], s5_b_re[0],
                            s5_b_im[0], s5_c_re[0], s5_c_im[0], s5_d[0], s5_glu_w[0], s5_glu_b[0], odd_w_out[0])
    x_lat = moe_layer(dims, x_lat, dims.m_lat, mod, norm2_g[1], router_w[1], router_b[1], moe_w1[1], moe_b1[1],
                      moe_w2[1], moe_b2[1], final_g=final_g)
    return x_lat.reshape(B, T, D)
```

```python
import functools
import math

import jax
import jax.numpy as jnp
import numpy as np
from jax import lax
from jax.experimental import pallas as pl
from jax.experimental.pallas import tpu as pltpu

F32 = jnp.float32
BF16 = jnp.bfloat16

NORM_EPS = 1e-6
GRID_W = 64
N_MOD = 6
LANES = 128
SUBLANES = 8
VMEM_BYTES = 64 << 20

GLA_HEADS = 4
GLA_DK = 128
GLA_DV = 256
GLA_RANK = 16
GLA_NORMALIZER = 16.0
GLA_CHUNK = 64
CONV_WIDTH = 31
CONV_PAD = 16
TOKEN_BLOCK = 256

RWKV_HEAD = 64
RWKV_GN_EPS = 64e-5
S5_GROUP = 16
S5_STATE = 64
S5_MAX_RE = -1e-4

N_EXPERTS = 32
TOP_K = 4
SWIGLU_ALPHA = 1.702
SWIGLU_LIMIT = 7.0


def _params(semantics, vmem_mb):
    return pltpu.CompilerParams(dimension_semantics=semantics, vmem_limit_bytes=vmem_mb << 20)


def _bdot(a, b):
    return jnp.dot(a.astype(BF16), b.astype(BF16), preferred_element_type=F32)


def _split2(x):
    hi = x.astype(BF16)
    lo = (x - hi.astype(F32)).astype(BF16)
    return hi, lo


def _dot3(a, b):
    ah, al = _split2(a)
    bh, bl = _split2(b)
    d = functools.partial(jnp.dot, preferred_element_type=F32)
    return d(ah, bh) + d(al, bh) + d(ah, bl)


def _dot_sel(sel, b):
    b1 = b.astype(BF16)
    r1 = b - b1.astype(F32)
    b2 = r1.astype(BF16)
    b3 = (r1 - b2.astype(F32)).astype(BF16)
    d = functools.partial(jnp.dot, preferred_element_type=F32)
    return d(sel, b1) + d(sel, b2) + d(sel, b3)


def _dot_rsel(a, sel):
    a1 = a.astype(BF16)
    r1 = a - a1.astype(F32)
    a2 = r1.astype(BF16)
    a3 = (r1 - a2.astype(F32)).astype(BF16)
    d = functools.partial(jnp.dot, preferred_element_type=F32)
    return d(a1, sel) + d(a2, sel) + d(a3, sel)


def _sigmoid(x):
    return 1.0 / (1.0 + jnp.exp(-x))


def _silu(x):
    return x * _sigmoid(x)


def _log_sigmoid(x):
    return jnp.minimum(x, 0.0) - jnp.log(1.0 + jnp.exp(-jnp.abs(x)))


def _softplus(x):
    return jnp.maximum(x, 0.0) + jnp.log(1.0 + jnp.exp(-jnp.abs(x)))


class Dims:
    def __init__(self, batch, seq, ctx_len):
        self.B, self.T, self.C = batch, seq, ctx_len
        self.m_lat = batch * seq
        self.m_ctx = batch * ctx_len
        self.m_all = self.m_lat + self.m_ctx
        assert ctx_len == TOKEN_BLOCK and seq % TOKEN_BLOCK == 0 and seq % GRID_W == 0
        self.nlb = seq // TOKEN_BLOCK
        self.tm = math.gcd(math.gcd(seq, self.m_ctx), 1024)

    def sample_of_tile(self, i, tm):
        return jnp.where(i * tm < self.m_lat, (i * tm) // self.T, self.B)


def _adaln_kernel(c_ref, w_ref, b_ref, o_ref):
    o_ref[...] = _bdot(_silu(c_ref[...]), w_ref[...]) + b_ref[...]


def adaln_table(cvec, ada_w, ada_b):
    depth, d, n = ada_w.shape
    tn = 1024
    return pl.pallas_call(
        _adaln_kernel,
        grid=(depth, n // tn),
        in_specs=[
            pl.BlockSpec((SUBLANES, d), lambda l, j: (0, 0)),
            pl.BlockSpec((None, d, tn), lambda l, j: (l, 0, j)),
            pl.BlockSpec((None, 1, tn), lambda l, j: (l, 0, j)),
        ],
        out_specs=pl.BlockSpec((None, SUBLANES, tn), lambda l, j: (l, 0, j)),
        out_shape=jax.ShapeDtypeStruct((depth, SUBLANES, n), F32),
        compiler_params=_params(("parallel", "parallel"), 40),
        name="adaln_table",
    )(cvec, ada_w, ada_b.reshape(depth, 1, n))


def _normmod(x, g, shift, scale):
    y = x * lax.rsqrt(jnp.mean(x * x, axis=-1, keepdims=True) + NORM_EPS) * g
    return y * (1.0 + scale) + shift


def _normmod_mm_kernel(x_ref, g_ref, sh_ref, sc_ref, w_ref, o_ref, h_ref):
    @pl.when(pl.program_id(1) == 0)
    def _():
        h_ref[...] = _normmod(x_ref[...], g_ref[...], sh_ref[...], sc_ref[...]).astype(BF16)

    o_ref[...] = jnp.dot(h_ref[...], w_ref[...].astype(BF16), preferred_element_type=F32)


def normmod_matmul(dims, x_all, g, shift, scale, w, tn):
    m, d = x_all.shape
    n = w.shape[1]
    tm = dims.tm
    samp = lambda i, j: (dims.sample_of_tile(i, tm), 0, 0)
    return pl.pallas_call(
        _normmod_mm_kernel,
        grid=(m // tm, n // tn),
        in_specs=[
            pl.BlockSpec((tm, d), lambda i, j: (i, 0)),
            pl.BlockSpec((1, d), lambda i, j: (0, 0)),
            pl.BlockSpec((None, 1, d), samp),
            pl.BlockSpec((None, 1, d), samp),
            pl.BlockSpec((d, tn), lambda i, j: (0, j)),
        ],
        out_specs=pl.BlockSpec((tm, tn), lambda i, j: (i, j)),
        out_shape=jax.ShapeDtypeStruct((m, n), F32),
        scratch_shapes=[pltpu.VMEM((tm, d), BF16)],
        compiler_params=_params(("parallel", "arbitrary"), 56),
        name="normmod_matmul",
    )(x_all, g.reshape(1, d), shift, scale, w)


EVEN_Q, EVEN_K, EVEN_V, EVEN_OG, EVEN_CA, EVEN_CB, EVEN_LRF, EVEN_LRB = 0, 512, 1024, 2048, 3072, 4096, 5120, 5248
EVEN_COLS = 5376


def _gla_chunk(q, k, v, lr, gw, gb, st_ref, forward):
    length = q.shape[0]
    row = lax.broadcasted_iota(jnp.int32, (length, length), 0)
    col = lax.broadcasted_iota(jnp.int32, (length, length), 1)
    tri = (col <= row) if forward else (col >= row)
    g = _log_sigmoid(_dot3(lr, gw) + gb) * (1.0 / GLA_NORMALIZER)
    cum = _dot_sel(tri.astype(BF16), g)
    tot = cum[length - 1:length, :] if forward else cum[0:1, :]
    q_dec = (q * (GLA_DK ** -0.5) * jnp.exp(cum)).astype(BF16)
    k_inv = (k * jnp.exp(-cum)).astype(BF16)
    k_st = (k * jnp.exp(tot - cum)).astype(BF16)
    scores = lax.dot_general(q_dec, k_inv, (((1,), (1,)), ((), ())), preferred_element_type=F32)
    scores = jnp.where(tri, scores, 0.0)
    st = st_ref[...]
    o = _bdot(scores, v) + lax.dot_general(q_dec, st.astype(BF16), (((1,), (1,)), ((), ())),
                                           preferred_element_type=F32)
    st_ref[...] = st * jnp.exp(tot) + lax.dot_general(
        v.astype(BF16), k_st, (((0,), (0,)), ((), ())), preferred_element_type=F32)
    return o


def _gla_kernel(qf, kf, vf, lf, qb, kb, vb, lb, gwf, gbf, gwb, gbb, of, ob, sf, sb):
    @pl.when(pl.program_id(2) == 0)
    def _():
        sf[...] = jnp.zeros_like(sf)
        sb[...] = jnp.zeros_like(sb)

    nchunk = TOKEN_BLOCK // GLA_CHUNK
    for c in range(nchunk):
        r = pl.ds(c * GLA_CHUNK, GLA_CHUNK)
        of[r, :] = _gla_chunk(qf[r, :], kf[r, :], vf[r, :], lf[r, :], gwf[...], gbf[...], sf, True)
        r = pl.ds((nchunk - 1 - c) * GLA_CHUNK, GLA_CHUNK)
        ob[r, :] = _gla_chunk(qb[r, :], kb[r, :], vb[r, :], lb[r, :], gwb[...], gbb[...], sb, False)


def gla_scan(dims, proj, gate_w_pad, gate_b):
    B, nlb = dims.B, dims.nlb
    blk = TOKEN_BLOCK

    def fblk(b, s):
        return jnp.where(s == 0, B * nlb + b, b * nlb + s - 1)

    def bblk(b, s):
        return jnp.where(s == 0, B * nlb + b, b * nlb + nlb - s)

    def col(blk_fn, width, start):
        return pl.BlockSpec((blk, width), lambda b, h, s: (blk_fn(b, s), start // width + h))

    def lr(blk_fn, start):
        return pl.BlockSpec((blk, LANES), lambda b, h, s: (blk_fn(b, s), start // LANES))

    def gw(d):
        return pl.BlockSpec((None, LANES, GLA_DK), lambda b, h, s: (d, 0, h))

    def gb(d):
        return pl.BlockSpec((None, 1, GLA_DK), lambda b, h, s: (d, 0, h))

    out = jax.ShapeDtypeStruct((dims.m_all, GLA_HEADS * GLA_DV), F32)
    return pl.pallas_call(
        _gla_kernel,
        grid=(B, GLA_HEADS, nlb + 1),
        in_specs=[
            col(fblk, GLA_DK, EVEN_Q), col(fblk, GLA_DK, EVEN_K), col(fblk, GLA_DV, EVEN_V), lr(fblk, EVEN_LRF),
            col(bblk, GLA_DK, EVEN_Q), col(bblk, GLA_DK, EVEN_K), col(bblk, GLA_DV, EVEN_V), lr(bblk, EVEN_LRB),
            gw(0), gb(0), gw(1), gb(1),
        ],
        out_specs=[
            pl.BlockSpec((blk, GLA_DV), lambda b, h, s: (fblk(b, s), h)),
            pl.BlockSpec((blk, GLA_DV), lambda b, h, s: (bblk(b, s), h)),
        ],
        out_shape=[out, out],
        scratch_shapes=[pltpu.VMEM((GLA_DV, GLA_DK), F32), pltpu.VMEM((GLA_DV, GLA_DK), F32)],
        compiler_params=_params(("parallel", "parallel", "arbitrary"), 32),
        name="gla_scan",
    )(proj, proj, proj, proj, proj, proj, proj, proj, gate_w_pad, gate_b, gate_w_pad, gate_b)


def _conv_kernel(ca_ref, cb_ref, w_ref, b_ref, g_ref, beta_ref, o_ref, pad_ref, *, row_len):
    blk, ch = o_ref.shape
    seg = row_len + 2 * CONV_PAD
    nrows = blk // row_len
    h = ca_ref[...] * _sigmoid(cb_ref[...])
    for r in range(nrows):
        pad_ref[pl.ds(r * seg, CONV_PAD), :] = jnp.zeros((CONV_PAD, ch), F32)
        pad_ref[pl.ds(r * seg + CONV_PAD, row_len), :] = h[r * row_len:(r + 1) * row_len, :]
        pad_ref[pl.ds(r * seg + CONV_PAD + row_len, CONV_PAD), :] = jnp.zeros((CONV_PAD, ch), F32)
    half = CONV_WIDTH // 2
    sub = 64
    cw = 256
    for r in range(nrows):
        for m in range(row_len // sub):
            for c in range(ch // cw):
                cs = pl.ds(c * cw, cw)
                acc = jnp.zeros((sub, cw), F32)
                base = r * seg + CONV_PAD + m * sub - half
                for j in range(CONV_WIDTH):
                    acc = acc + w_ref[pl.ds(j, 1), cs] * pad_ref[pl.ds(base + j, sub), cs]
                o_ref[pl.ds(r * row_len + m * sub, sub), cs] = acc
    hf = o_ref[...] + b_ref[...]
    mean = jnp.mean(hf, axis=-1, keepdims=True)
    cen = hf - mean
    var = jnp.mean(cen * cen, axis=-1, keepdims=True)
    o_ref[...] = _silu(cen * lax.rsqrt(var + NORM_EPS) * g_ref[...] + beta_ref[...])


def conformer_conv(dims, proj, conv_w, conv_b, ln_g, ln_b):
    ch = conv_w.shape[1]
    blk = TOKEN_BLOCK
    n_lat = dims.m_lat // blk

    def run(row_len, nblk, blk0):
        vec = lambda: pl.BlockSpec((1, ch), lambda i: (0, 0))
        return pl.pallas_call(
            functools.partial(_conv_kernel, row_len=row_len),
            grid=(nblk,),
            in_specs=[
                pl.BlockSpec((blk, ch), lambda i: (blk0 + i, EVEN_CA // ch)),
                pl.BlockSpec((blk, ch), lambda i: (blk0 + i, EVEN_CB // ch)),
                pl.BlockSpec((CONV_WIDTH, ch), lambda i: (0, 0)),
                vec(), vec(), vec(),
            ],
            out_specs=pl.BlockSpec((blk, ch), lambda i: (i, 0)),
            out_shape=jax.ShapeDtypeStruct((nblk * blk, ch), F32),
            scratch_shapes=[pltpu.VMEM((blk // row_len * (row_len + 2 * CONV_PAD), ch), F32)],
            compiler_params=_params(("parallel",), 32),
            name="conformer_conv",
        )(proj, proj, conv_w, conv_b.reshape(1, ch), ln_g.reshape(1, ch), ln_b.reshape(1, ch))

    lat = run(GRID_W, n_lat, 0)
    ctx = run(dims.C, dims.m_ctx // blk, n_lat)
    return jnp.concatenate([lat, ctx], axis=0)


def _even_out_kernel(x_ref, of_ref, ob_ref, og_ref, cv_ref, ng_ref, gate_ref, w_ref, o_ref, a_ref):
    @pl.when(pl.program_id(1) == 0)
    def _():
        for h in range(GLA_HEADS):
            cs = pl.ds(h * GLA_DV, GLA_DV)
            o = of_ref[:, cs] + ob_ref[:, cs]
            y = o * lax.rsqrt(jnp.mean(o * o, axis=-1, keepdims=True) + NORM_EPS) * ng_ref[...]
            a_ref[:, cs] = (y * _silu(og_ref[:, cs])).astype(BF16)
        nv = GLA_HEADS * GLA_DV
        a_ref[:, pl.ds(nv, cv_ref.shape[1])] = cv_ref[...].astype(BF16)

    m = jnp.dot(a_ref[...], w_ref[...].astype(BF16), preferred_element_type=F32)
    o_ref[...] = x_ref[...] + gate_ref[...] * m


def even_out(dims, x_all, proj, o_f, o_b, conv_out, gla_norm_g, w_out, gate, tn=512):
    m, d = x_all.shape
    tm = dims.tm
    nv = GLA_HEADS * GLA_DV
    return pl.pallas_call(
        _even_out_kernel,
        grid=(m // tm, d // tn),
        in_specs=[
            pl.BlockSpec((tm, tn), lambda i, j: (i, j)),
            pl.BlockSpec((tm, nv), lambda i, j: (i, 0)),
            pl.BlockSpec((tm, nv), lambda i, j: (i, 0)),
            pl.BlockSpec((tm, nv), lambda i, j: (i, EVEN_OG // nv)),
            pl.BlockSpec((tm, d - nv), lambda i, j: (i, 0)),
            pl.BlockSpec((1, GLA_DV), lambda i, j: (0, 0)),
            pl.BlockSpec((None, 1, tn), lambda i, j: (dims.sample_of_tile(i, tm), 0, j)),
            pl.BlockSpec((d, tn), lambda i, j: (0, j)),
        ],
        out_specs=pl.BlockSpec((tm, tn), lambda i, j: (i, j)),
        out_shape=jax.ShapeDtypeStruct((m, d), F32),
        scratch_shapes=[pltpu.VMEM((tm, d), BF16)],
        compiler_params=_params(("parallel", "arbitrary"), 56),
        name="even_out",
    )(x_all, o_f, o_b, proj, conv_out, gla_norm_g.reshape(1, GLA_DV), gate, w_out)


def even_weights(w_in, gate_w):
    d = w_in.shape[0]
    kw, vw = GLA_HEADS * GLA_DK, GLA_HEADS * GLA_DV
    main = 2 * kw + 2 * vw
    lr_f = w_in[:, main:main + GLA_RANK]
    lr_b = w_in[:, main + GLA_RANK:main + 2 * GLA_RANK]
    zpad = jnp.zeros((d, LANES - GLA_RANK), F32)
    w_pad = jnp.concatenate([w_in[:, :main], w_in[:, main + 2 * GLA_RANK:], lr_f, zpad, lr_b, zpad], axis=1)
    assert w_pad.shape[1] == EVEN_COLS
    gw_pad = jnp.pad(gate_w, ((0, 0), (0, LANES - GLA_RANK), (0, 0)))
    return w_pad, gw_pad


def even_mixer_layer(dims, x_all, mod, norm_g, w_in, gate_w, gate_b, gla_norm_g, conv_w, conv_b, conv_ln_g,
                     conv_ln_b, w_out):
    w_pad, gw_pad = even_weights(w_in, gate_w)
    proj = normmod_matmul(dims, x_all, norm_g, mod[0], mod[1], w_pad, tn=768)
    o_f, o_b = gla_scan(dims, proj, gw_pad, gate_b.reshape(2, 1, -1))
    conv_out = conformer_conv(dims, proj, conv_w, conv_b, conv_ln_g, conv_ln_b)
    return even_out(dims, x_all, proj, o_f, o_b, conv_out, gla_norm_g, w_out, mod[2])


EXPERT_TILE = 512
UP_CHUNK = 1024
DOWN_CHUNK = 1024
ROW_TILE = 256
NEG_INF = float("-inf")


def _router_kernel(x_ref, g_ref, sh_ref, sc_ref, rw_ref, rb_ref, f_ref, ti_ref, tw_ref, pos_ref, cnt_ref, carry_ref):
    @pl.when(pl.program_id(0) == 0)
    def _():
        carry_ref[...] = jnp.zeros_like(carry_ref)

    f = _normmod(x_ref[...], g_ref[...], sh_ref[...], sc_ref[...])
    f_ref[...] = f
    tm = f.shape[0]
    work = _dot3(f, rw_ref[...]) + rb_ref[...]
    lane = lax.broadcasted_iota(jnp.int32, (tm, LANES), 1)
    vals, idxs = [], []
    for _ in range(TOP_K):
        m = jnp.max(work, axis=-1, keepdims=True)
        idx = jnp.min(jnp.where(work == m, lane, LANES), axis=-1, keepdims=True)
        vals.append(m)
        idxs.append(idx)
        work = jnp.where(lane == idx, NEG_INF, work)
    es = [jnp.exp(v - vals[0]) for v in vals]
    denom = es[0] + es[1] + es[2] + es[3]
    sel = (work == NEG_INF).astype(F32)
    row = lax.broadcasted_iota(jnp.int32, (tm, tm), 0)
    col = lax.broadcasted_iota(jnp.int32, (tm, tm), 1)
    before = jnp.dot((col < row).astype(BF16), sel.astype(BF16), preferred_element_type=F32)
    rank = before + carry_ref[...]
    ti = jnp.zeros((tm, LANES), jnp.int32)
    tw = jnp.zeros((tm, LANES), F32)
    pos = jnp.zeros((tm, LANES), jnp.int32)
    for k in range(TOP_K):
        pk = jnp.sum(jnp.where(lane == idxs[k], rank, 0.0), axis=-1, keepdims=True).astype(jnp.int32)
        ti = jnp.where(lane == k, idxs[k], ti)
        tw = jnp.where(lane == k, es[k] / denom, tw)
        pos = jnp.where(lane == k, pk, pos)
    ti_ref[...] = ti
    tw_ref[...] = tw
    pos_ref[...] = pos
    carry_ref[...] = carry_ref[...] + jnp.sum(sel, axis=0, keepdims=True)
    cnt_ref[...] = carry_ref[...]


def moe_router(dims, x_all, m_rows, g, shift, scale, router_w, router_b):
    d = x_all.shape[1]
    tm = min(dims.tm, 512)
    rw = jnp.pad(router_w, ((0, 0), (0, LANES - N_EXPERTS)))
    rb = jnp.concatenate([router_b, jnp.full((LANES - N_EXPERTS,), -1e30, F32)]).reshape(1, LANES)
    samp = lambda i: (dims.sample_of_tile(i, tm), 0, 0)
    lane_out = lambda dt: jax.ShapeDtypeStruct((m_rows, LANES), dt)
    lane_spec = lambda: pl.BlockSpec((tm, LANES), lambda i: (i, 0))
    return pl.pallas_call(
        _router_kernel,
        grid=(m_rows // tm,),
        in_specs=[
            pl.BlockSpec((tm, d), lambda i: (i, 0)),
            pl.BlockSpec((1, d), lambda i: (0, 0)),
            pl.BlockSpec((None, 1, d), samp),
            pl.BlockSpec((None, 1, d), samp),
            pl.BlockSpec((d, LANES), lambda i: (0, 0)),
            pl.BlockSpec((1, LANES), lambda i: (0, 0)),
        ],
        out_specs=[pl.BlockSpec((tm, d), lambda i: (i, 0)), lane_spec(), lane_spec(), lane_spec(),
                   pl.BlockSpec((1, LANES), lambda i: (0, 0))],
        out_shape=[jax.ShapeDtypeStruct((m_rows, d), F32), lane_out(jnp.int32), lane_out(F32), lane_out(jnp.int32),
                   jax.ShapeDtypeStruct((1, LANES), F32)],
        scratch_shapes=[pltpu.VMEM((1, LANES), F32)],
        compiler_params=_params(("arbitrary",), 40),
        name="moe_router",
    )(x_all, g.reshape(1, d), shift, scale, rw, rb)


def _dispatch_kernel(slot_ref, f_ref, xs_in, xs_hbm, sem):
    del xs_in

    def copy(r, k):
        return pltpu.make_async_copy(f_ref.at[pl.ds(r, 1)], xs_hbm.at[pl.ds(slot_ref[r * TOP_K + k], 1)], sem)

    def issue(r, carry):
        for k in range(TOP_K):
            copy(r, k).start()
        return carry

    def drain(r, carry):
        for k in range(TOP_K):
            copy(r, k).wait()
        return carry

    lax.fori_loop(0, ROW_TILE, issue, 0)
    lax.fori_loop(0, ROW_TILE, drain, 0)


def moe_dispatch(f, slot_flat, n_slots):
    m, d = f.shape
    xs0 = jnp.zeros((n_slots, d), F32)
    return pl.pallas_call(
        _dispatch_kernel,
        grid=(m // ROW_TILE,),
        in_specs=[
            pl.BlockSpec((ROW_TILE * TOP_K,), lambda i: (i,), memory_space=pltpu.SMEM),
            pl.BlockSpec((ROW_TILE, d), lambda i: (i, 0)),
            pl.BlockSpec(memory_space=pl.ANY),
        ],
        out_specs=pl.BlockSpec(memory_space=pl.ANY),
        out_shape=jax.ShapeDtypeStruct((n_slots, d), F32),
        scratch_shapes=[pltpu.SemaphoreType.DMA(())],
        input_output_aliases={2: 0},
        compiler_params=_params(("arbitrary",), 16),
        name="moe_dispatch",
    )(slot_flat, f, xs0)


def _expert_changed(te_ref):
    t = pl.program_id(1)
    return jnp.logical_or(t == 0, te_ref[t] != te_ref[jnp.maximum(t - 1, 0)])


def _expert_up_kernel(te_ref, tv_ref, x_ref, w_ref, b_ref, h_ref, wb_ref):
    half = UP_CHUNK // 2

    @pl.when(_expert_changed(te_ref))
    def _():
        wb_ref[...] = w_ref[...].astype(BF16)

    @pl.when(tv_ref[pl.program_id(1)] != 0)
    def _():
        h = jnp.dot(x_ref[...].astype(BF16), wb_ref[...], preferred_element_type=F32) + b_ref[...]
        even = lax.broadcasted_iota(jnp.int32, (h.shape[0], half), 1) % 2 == 0

        def act(hh):
            glu = jnp.minimum(hh, SWIGLU_LIMIT)
            lin = jnp.clip(hh, -SWIGLU_LIMIT, SWIGLU_LIMIT) + 1.0
            return glu * _sigmoid(SWIGLU_ALPHA * glu) * pltpu.roll(lin, half - 1, axis=1)

        packed = jnp.where(even, act(h[:, :half]), pltpu.roll(act(h[:, half:]), 1, axis=1))
        h_ref[...] = packed.astype(BF16)

    @pl.when(tv_ref[pl.program_id(1)] == 0)
    def _():
        h_ref[...] = jnp.zeros_like(h_ref)


def _expert_down_kernel(te_ref, tv_ref, h_ref, w_ref, b_ref, y_ref, stage_ref, wb_ref):
    half = UP_CHUNK // 4

    @pl.when(_expert_changed(te_ref))
    def _():
        for c in range(stage_ref.shape[0]):
            cols = pl.ds(c * LANES, LANES)
            for blk in range(w_ref.shape[0] // (2 * half)):
                for s in range(2):
                    stage_ref[c, pl.ds(blk * 2 * half + s, half, stride=2), :] = w_ref[
                        pl.ds((2 * blk + s) * half, half), cols]
            wb_ref[:, cols] = stage_ref[c].astype(BF16)

    @pl.when(tv_ref[pl.program_id(1)] != 0)
    def _():
        y_ref[...] = jnp.dot(h_ref[...], wb_ref[...], preferred_element_type=F32) + b_ref[...]

    @pl.when(tv_ref[pl.program_id(1)] == 0)
    def _():
        y_ref[...] = jnp.zeros_like(y_ref)


def moe_experts(xs, tile_expert, tile_valid, w1, b1, w2, b2):
    n_slots, d = xs.shape
    n_exp, _, two_f = w1.shape
    f = two_f // 2
    nt = n_slots // EXPERT_TILE
    grid_spec = pltpu.PrefetchScalarGridSpec(
        num_scalar_prefetch=2,
        grid=(two_f // UP_CHUNK, nt),
        in_specs=[
            pl.BlockSpec((EXPERT_TILE, d), lambda j, t, te, tv: (t, 0)),
            pl.BlockSpec((None, d, UP_CHUNK), lambda j, t, te, tv: (te[t], 0, j)),
            pl.BlockSpec((None, 1, UP_CHUNK), lambda j, t, te, tv: (te[t], 0, j)),
        ],
        out_specs=pl.BlockSpec((EXPERT_TILE, UP_CHUNK // 2), lambda j, t, te, tv: (t, j)),
        scratch_shapes=[pltpu.VMEM((d, UP_CHUNK), BF16)],
    )
    hidden = pl.pallas_call(
        _expert_up_kernel,
        grid_spec=grid_spec,
        out_shape=jax.ShapeDtypeStruct((n_slots, f), BF16),
        compiler_params=_params(("arbitrary", "arbitrary"), 48),
        name="moe_expert_up",
    )(tile_expert, tile_valid, xs, w1, b1.reshape(n_exp, 1, two_f))
    grid_spec = pltpu.PrefetchScalarGridSpec(
        num_scalar_prefetch=2,
        grid=(d // DOWN_CHUNK, nt),
        in_specs=[
            pl.BlockSpec((EXPERT_TILE, f), lambda j, t, te, tv: (t, 0)),
            pl.BlockSpec((None, f, DOWN_CHUNK), lambda j, t, te, tv: (te[t], 0, j)),
            pl.BlockSpec((None, 1, DOWN_CHUNK), lambda j, t, te, tv: (te[t], 0, j)),
        ],
        out_specs=pl.BlockSpec((EXPERT_TILE, DOWN_CHUNK), lambda j, t, te, tv: (t, j)),
        scratch_shapes=[pltpu.VMEM((DOWN_CHUNK // LANES, f, LANES), F32), pltpu.VMEM((f, DOWN_CHUNK), BF16)],
    )
    return pl.pallas_call(
        _expert_down_kernel,
        grid_spec=grid_spec,
        out_shape=jax.ShapeDtypeStruct((n_slots, d), F32),
        compiler_params=_params(("arbitrary", "arbitrary"), 56),
        name="moe_expert_down",
    )(tile_expert, tile_valid, hidden, w2, b2.reshape(n_exp, 1, d))


def _combine_kernel(slot_ref, x_ref, tw_ref, gate_ref, fg_ref, ys_hbm, o_ref, buf, sem, *, final_norm):
    def copy(r, k):
        return pltpu.make_async_copy(ys_hbm.at[pl.ds(slot_ref[r * TOP_K + k], 1)], buf.at[k, pl.ds(r, 1)], sem)

    def issue(r, carry):
        for k in range(TOP_K):
            copy(r, k).start()
        return carry

    def drain(r, carry):
        for k in range(TOP_K):
            copy(r, k).wait()
        return carry

    lax.fori_loop(0, ROW_TILE, issue, 0)
    lax.fori_loop(0, ROW_TILE, drain, 0)
    tw = tw_ref[...]
    y = tw[:, 0:1] * buf[0]
    for k in range(1, TOP_K):
        y = y + tw[:, k:k + 1] * buf[k]
    out = x_ref[...] + gate_ref[...] * y
    if final_norm:
        out = out * lax.rsqrt(jnp.mean(out * out, axis=-1, keepdims=True) + NORM_EPS) * fg_ref[...]
    o_ref[...] = out


def moe_combine(dims, x_all, m_rows, slot_flat, top_w, gate, ys, final_g):
    d = x_all.shape[1]
    tm = ROW_TILE
    fg = jnp.ones((1, d), F32) if final_g is None else final_g.reshape(1, d)
    return pl.pallas_call(
        functools.partial(_combine_kernel, final_norm=final_g is not None),
        grid=(m_rows // tm,),
        in_specs=[
            pl.BlockSpec((tm * TOP_K,), lambda i: (i,), memory_space=pltpu.SMEM),
            pl.BlockSpec((tm, d), lambda i: (i, 0)),
            pl.BlockSpec((tm, LANES), lambda i: (i, 0)),
            pl.BlockSpec((None, 1, d), lambda i: (dims.sample_of_tile(i, tm), 0, 0)),
            pl.BlockSpec((1, d), lambda i: (0, 0)),
            pl.BlockSpec(memory_space=pl.ANY),
        ],
        out_specs=pl.BlockSpec((tm, d), lambda i: (i, 0)),
        out_shape=jax.ShapeDtypeStruct((m_rows, d), F32),
        scratch_shapes=[pltpu.VMEM((TOP_K, tm, d), F32), pltpu.SemaphoreType.DMA(())],
        compiler_params=_params(("arbitrary",), 40),
        name="moe_combine",
    )(slot_flat, x_all, top_w, gate, fg, ys)


def moe_layer(dims, x_all, m_rows, mod, norm_g, router_w, router_b, w1, b1, w2, b2, final_g=None):
    n_exp, f, d = w2.shape
    f_tok, top_i, top_w, pos, cnt = moe_router(dims, x_all, m_rows, norm_g, mod[3], mod[4], router_w, router_b)
    counts = cnt[0, :n_exp].astype(jnp.int32)
    tiles_e = (counts + EXPERT_TILE - 1) // EXPERT_TILE
    tile_end = jnp.cumsum(tiles_e)
    offset = (tile_end - tiles_e) * EXPERT_TILE
    slot = (offset[top_i[:, :TOP_K]] + pos[:, :TOP_K]).reshape(-1)
    nt = m_rows * TOP_K // EXPERT_TILE + n_exp
    tile_ids = jnp.arange(nt, dtype=jnp.int32)
    tile_expert = jnp.minimum(jnp.sum(tile_ids[:, None] >= tile_end[None, :], axis=1), n_exp - 1).astype(jnp.int32)
    tile_valid = (tile_ids < tile_end[-1]).astype(jnp.int32)
    xs = moe_dispatch(f_tok, slot, nt * EXPERT_TILE)
    ys = moe_experts(xs, tile_expert, tile_valid, w1, b1, w2, b2)
    return moe_combine(dims, x_all, m_rows, slot, top_w, mod[5], ys, final_g)


ODD_R, ODD_K, ODD_V, ODD_WL, ODD_AL, ODD_GL, ODD_S5 = 0, 1536, 3072, 4608, 4736, 4864, 4992
ODD_COLS = 5632
RWKV_WIDTH = 1536
RWKV_HEADS = RWKV_WIDTH // RWKV_HEAD
RWKV_RANK = 64
CHAIN_GROUP = 64
SCAN_STEPS = 8


def _head_sum_matrices():
    ch = np.arange(RWKV_WIDTH)[:, None] // RWKV_HEAD
    e = (ch == np.arange(LANES)[None, :]).astype(np.float32)
    return jnp.asarray(e, BF16), jnp.asarray(e.T, BF16)


def _rwkv_prep_kernel(r_ref, k_ref, v_ref, wl_ref, al_ref, rn_ref, kn_ref, vn_ref, wln_ref, aln_ref,
                      mur, muk, muv, muwl, mual, w0, w2, a0, a2, kk_w, ka_w, rk_w, e_ref, et_ref,
                      dec_o, kk_o, b_o, k_o, wr_o, v_o, bonus_o, br_o, kr_o, *, forward, nlb, n_lat_blocks):
    i = pl.program_id(0)
    tm = r_ref.shape[0]
    is_lat = i < n_lat_blocks
    if forward:
        edge = jnp.logical_or(jnp.logical_not(is_lat), i % nlb == 0)
    else:
        edge = jnp.logical_or(jnp.logical_not(is_lat), i % nlb == nlb - 1)
    keep = jnp.where(edge, 0.0, 1.0)

    def mix(x_ref, n_ref, mu_ref):
        x = x_ref[...]
        rows = lax.broadcasted_iota(jnp.int32, x.shape, 0)
        if forward:
            sh = jnp.where(rows == 0, n_ref[SUBLANES - 1:SUBLANES, :] * keep, pltpu.roll(x, 1, axis=0))
        else:
            sh = jnp.where(rows == tm - 1, n_ref[0:1, :] * keep, pltpu.roll(x, tm - 1, axis=0))
        return x + (sh - x) * mu_ref[...]

    e = e_ref[...]
    et = et_ref[...]
    r = mix(r_ref, rn_ref, mur)
    k = mix(k_ref, kn_ref, muk)
    v = mix(v_ref, vn_ref, muv)
    wl = mix(wl_ref, wln_ref, muwl)
    al = mix(al_ref, aln_ref, mual)
    w = -_softplus(-(w0[...] + _dot3(jnp.tanh(wl), w2[...]))) - 0.5
    decay = jnp.exp(-jnp.exp(w))
    a = _sigmoid(a0[...] + _dot3(al, a2[...]))
    kk = k * kk_w[...]
    inv = 1.0 / jnp.maximum(jnp.sqrt(_dot_rsel(kk * kk, e)), 1e-12)
    kk = kk * _dot_rsel(inv, et)
    k = k * (1.0 + (a - 1.0) * ka_w[...])
    b = kk * a
    dec_o[...] = decay
    kk_o[...] = kk
    b_o[...] = b
    k_o[...] = k
    wr_o[...] = decay * r
    v_o[...] = v
    br_o[...] = _dot_rsel(b * r, e)
    kr_o[...] = _dot_rsel(k * r, e)
    bonus_o[...] = _dot_rsel(_dot_rsel(r * k * rk_w[...], e), et) * v


def rwkv_prep(dims, proj, direction, mu, w0, w2, a0, a2, k_k, k_a, r_k):
    wd = RWKV_WIDTH
    tm = TOKEN_BLOCK
    nblk = dims.m_all // tm
    forward = direction == 0
    per8 = tm // SUBLANES
    last8 = dims.m_all // SUBLANES - 1

    def nb(i):
        return jnp.maximum(i * per8 - 1, 0) if forward else jnp.minimum((i + 1) * per8, last8)

    def main(width, start):
        return pl.BlockSpec((tm, width), lambda i: (i, start // width))

    def neigh(width, start):
        return pl.BlockSpec((SUBLANES, width), lambda i: (nb(i), start // width))

    vec = lambda width: pl.BlockSpec((1, width), lambda i: (0, 0))
    lo = direction * RWKV_RANK
    lane_pad = lambda x: jnp.zeros((LANES,) + x.shape[1:], F32).at[lo:lo + RWKV_RANK].set(x)
    mu_d = mu[direction]
    e, et = _head_sum_matrices()
    n_lat = dims.m_lat // tm
    steps = dims.C + dims.T

    def seq_pos(i):
        is_lat = i < n_lat
        sample = jnp.where(is_lat, i // dims.nlb, i - n_lat)
        if forward:
            return sample, jnp.where(is_lat, 1 + i % dims.nlb, 0), 0
        return sample, jnp.where(is_lat, i % dims.nlb, dims.nlb), 0

    seq_spec = lambda width: pl.BlockSpec((None, tm, width), seq_pos)
    wide = jax.ShapeDtypeStruct((dims.B, steps, wd), F32)
    lane = jax.ShapeDtypeStruct((dims.B, steps, LANES), F32)
    return pl.pallas_call(
        functools.partial(_rwkv_prep_kernel, forward=forward, nlb=dims.nlb, n_lat_blocks=dims.m_lat // tm),
        grid=(nblk,),
        in_specs=[main(wd, ODD_R), main(wd, ODD_K), main(wd, ODD_V), main(LANES, ODD_WL), main(LANES, ODD_AL),
                  neigh(wd, ODD_R), neigh(wd, ODD_K), neigh(wd, ODD_V), neigh(LANES, ODD_WL), neigh(LANES, ODD_AL),
                  vec(wd), vec(wd), vec(wd), vec(LANES), vec(LANES),
                  vec(wd), pl.BlockSpec((LANES, wd), lambda i: (0, 0)),
                  vec(wd), pl.BlockSpec((LANES, wd), lambda i: (0, 0)),
                  vec(wd), vec(wd), vec(wd),
                  pl.BlockSpec((wd, LANES), lambda i: (0, 0)), pl.BlockSpec((LANES, wd), lambda i: (0, 0))],
        out_specs=[seq_spec(wd)] * 6 + [pl.BlockSpec((tm, wd), lambda i: (i, 0))] + [seq_spec(LANES)] * 2,
        out_shape=[wide] * 6 + [jax.ShapeDtypeStruct((dims.m_all, wd), F32)] + [lane] * 2,
        compiler_params=_params(("parallel",), 56),
        name="rwkv_prep",
    )(proj, proj, proj, proj, proj, proj, proj, proj, proj, proj,
      mu_d[0:wd].reshape(1, wd), mu_d[wd:2 * wd].reshape(1, wd), mu_d[2 * wd:3 * wd].reshape(1, wd),
      lane_pad(mu_d[3 * wd:3 * wd + RWKV_RANK]).reshape(1, LANES),
      lane_pad(mu_d[3 * wd + RWKV_RANK:3 * wd + 2 * RWKV_RANK]).reshape(1, LANES),
      w0[direction].reshape(1, wd), lane_pad(w2[direction]), a0[direction].reshape(1, wd), lane_pad(a2[direction]),
      k_k.reshape(1, wd), k_a.reshape(1, wd), r_k.reshape(1, wd), e, et)


def _rwkv_scan_kernel(*refs):
    fwd, bwd = refs[0:8], refs[8:16]
    yf_ref, yb_ref, st_ref, bc_ref = refs[16:20]

    @pl.when(pl.program_id(0) == 0)
    def _():
        st_ref[...] = jnp.zeros_like(st_ref)

    ngroups = st_ref.shape[0]
    nk = st_ref.shape[1]
    hv = st_ref.shape[2]
    nc = yf_ref.shape[2]
    pad = ngroups * CHAIN_GROUP - 2 * nc

    def dup(x):
        return jnp.concatenate([x, x], axis=-1)

    def both(n, t, tb):
        parts = [fwd[n][t], bwd[n][tb]]
        if pad:
            parts.append(jnp.zeros(parts[0].shape[:-1] + (pad,), F32))
        return jnp.concatenate(parts, axis=-1)

    def step(t, carry):
        tb = SCAN_STEPS - 1 - t
        vall, brall, krall = both(5, t, tb), both(6, t, tb), both(7, t, tb)
        feats = [both(n, t, tb) for n in range(5)]
        youts = []
        for g in range(ngroups):
            lanes = slice(g * CHAIN_GROUP, (g + 1) * CHAIN_GROUP)
            for n in range(5):
                bc_ref[n] = dup(feats[n][:, lanes])
            vfull = vall[:, lanes]
            v2 = jnp.concatenate([vfull[:hv], vfull[hv:]], axis=-1)
            br = jnp.broadcast_to(dup(brall[:, lanes]), (hv, LANES))
            kr = jnp.broadcast_to(dup(krall[:, lanes]), (hv, LANES))

            def row(n, k):
                return jnp.broadcast_to(bc_ref[n, pl.ds(k, 1), :], (hv, LANES))

            def reduce_k(k, acc):
                sa, z = acc
                s = st_ref[g, k]
                return sa - s * row(1, k), z + s * row(4, k)

            zero = jnp.zeros((hv, LANES), F32)
            sa, z = lax.fori_loop(0, nk, reduce_k, (zero, zero), unroll=8)
            y = z + sa * br + v2 * kr

            def update_k(k, c):
                st_ref[g, k] = st_ref[g, k] * row(0, k) + sa * row(2, k) + v2 * row(3, k)
                return c

            lax.fori_loop(0, nk, update_k, 0, unroll=8)
            youts.append(jnp.concatenate([y[:, :CHAIN_GROUP], y[:, CHAIN_GROUP:]], axis=0))
        yall = jnp.concatenate(youts, axis=-1)
        yf_ref[t] = yall[:, :nc]
        yb_ref[tb] = yall[:, nc:2 * nc]
        return carry

    lax.fori_loop(0, SCAN_STEPS, step, 0)


def rwkv_scan(fwd, bwd):
    steps, n, nc = fwd[5].shape
    nblk = steps // SCAN_STEPS
    ngroups = -(-2 * nc // CHAIN_GROUP)
    spec = lambda rows, rev: pl.BlockSpec((SCAN_STEPS, rows, nc), (lambda i: (nblk - 1 - i, 0, 0)) if rev
                                          else (lambda i: (i, 0, 0)))
    side = lambda rev: [spec(n, rev)] * 6 + [spec(1, rev)] * 2
    out = jax.ShapeDtypeStruct((steps, n, nc), F32)
    return pl.pallas_call(
        _rwkv_scan_kernel,
        grid=(nblk,),
        in_specs=side(False) + side(True),
        out_specs=[spec(n, False), spec(n, True)],
        out_shape=[out, out],
        scratch_shapes=[pltpu.VMEM((ngroups, n, n // 2, LANES), F32), pltpu.VMEM((5, n, LANES), F32)],
        compiler_params=_params(("arbitrary",), 32),
        name="rwkv_scan",
    )(*fwd, *bwd)


def _scan_order(dims, a, forward, col_major=False):
    width = a.shape[1]
    lat = a[:dims.m_lat].reshape(dims.B, dims.T, width)
    if col_major:
        lat = lat.reshape(dims.B, dims.T // GRID_W, GRID_W, width).transpose(0, 2, 1, 3).reshape(dims.B, dims.T, width)
    ctx = a[dims.m_lat:].reshape(dims.B, dims.C, width)
    if not forward:
        lat, ctx = lat[:, ::-1], ctx[:, ::-1]
    return jnp.concatenate([ctx, lat], axis=1)


def _to_chains(dims, a, per_head):
    steps = a.shape[1]
    a = a[:, :, :RWKV_HEADS * per_head].reshape(dims.B, steps, RWKV_HEADS, per_head).transpose(1, 3, 0, 2)
    return a.reshape(steps, per_head, dims.B * RWKV_HEADS)


def _from_chains(dims, y, first):
    y = y[first:first + dims.T].reshape(dims.T, RWKV_HEAD, dims.B, RWKV_HEADS).transpose(2, 0, 3, 1)
    return y.reshape(dims.m_lat, RWKV_WIDTH)


def rwkv_mixer(dims, proj, mu, w0, w2, a0, a2, k_k, k_a, r_k):
    sides, bonus = [], []
    for direction in range(2):
        p = rwkv_prep(dims, proj, direction, mu, w0, w2, a0, a2, k_k, k_a, r_k)
        sides.append([_to_chains(dims, p[n], RWKV_HEAD) for n in range(6)]
                     + [_to_chains(dims, p[7], 1), _to_chains(dims, p[8], 1)])
        bonus.append(p[6])
    y_f, y_b = rwkv_scan(*sides)
    return _from_chains(dims, y_f, dims.C), _from_chains(dims, y_b, 0), bonus[0], bonus[1]


S5_WIDTH = 512
S5_GROUPS = S5_WIDTH // S5_GROUP
S5_LANES = S5_GROUPS * S5_STATE
S5_BLOCKS = S5_WIDTH // LANES
S5_STEPS = 32


def _s5_kernel(u_ref, lr_ref, li_ref, wre_ref, wim_ref, cre_ref, cim_ref, y_ref, xr_ref, xi_ref, sr_ref, si_ref,
               *, n_fwd_rows):
    @pl.when(pl.program_id(0) == 0)
    def _():
        sr_ref[...] = jnp.zeros_like(sr_ref)
        si_ref[...] = jnp.zeros_like(si_ref)

    rows = S5_STEPS * SUBLANES
    seg = S5_LANES // S5_BLOCKS
    u = u_ref[...].reshape(rows, S5_WIDTH).astype(BF16)
    backward = lax.broadcasted_iota(jnp.int32, (rows, 1), 0) % SUBLANES >= n_fwd_rows
    for q in range(S5_BLOCKS):
        uq = u[:, q * LANES:(q + 1) * LANES]
        cols = pl.ds(q * seg, seg)
        dot = functools.partial(jnp.dot, preferred_element_type=F32)
        xr_ref[:, cols] = jnp.where(backward, dot(uq, wre_ref[1, q]), dot(uq, wre_ref[0, q]))
        xi_ref[:, cols] = jnp.where(backward, dot(uq, wim_ref[1, q]), dot(uq, wim_ref[0, q]))

    lam_r = lr_ref[...]
    lam_i = li_ref[...]

    def step(t, carry):
        sr, si = carry
        r = pl.ds(pl.multiple_of(t * SUBLANES, SUBLANES), SUBLANES)
        nr = lam_r * sr - lam_i * si + xr_ref[r, :]
        ni = lam_r * si + lam_i * sr + xi_ref[r, :]
        xr_ref[r, :] = nr
        xi_ref[r, :] = ni
        return nr, ni

    sr, si = lax.fori_loop(0, S5_STEPS, step, (sr_ref[...], si_ref[...]))
    sr_ref[...] = sr
    si_ref[...] = si
    for q in range(S5_BLOCKS):
        cols = pl.ds(q * seg, seg)
        yq = _bdot(xr_ref[:, cols], cre_ref[q]) - _bdot(xi_ref[:, cols], cim_ref[q])
        y_ref[:, :, pl.ds(q * LANES, LANES)] = yq.reshape(S5_STEPS, SUBLANES, LANES)


def _s5_maps(lam_re, lam_im, log_dt, b_re, b_im, c_re, c_im):
    lre = jnp.minimum(lam_re, S5_MAX_RE)
    dt = jnp.exp(log_dt)[..., None]
    mag = jnp.exp(lre * dt)
    lbr, lbi = mag * jnp.cos(lam_im * dt), mag * jnp.sin(lam_im * dt)
    den = lre * lre + lam_im * lam_im
    cr = ((lbr - 1.0) * lre + lbi * lam_im) / den
    ci = (lbi * lre - (lbr - 1.0) * lam_im) / den
    bbr = cr[..., None] * b_re[None] - ci[..., None] * b_im[None]
    bbi = cr[..., None] * b_im[None] + ci[..., None] * b_re[None]
    gpb = S5_GROUPS // S5_BLOCKS
    eye = jnp.eye(gpb, dtype=F32)

    def in_map(bb):
        bb = bb.reshape(2, S5_BLOCKS, gpb, S5_STATE, S5_GROUP)
        return jnp.einsum("dqgpc,gh->dqgchp", bb, eye).reshape(2, S5_BLOCKS, LANES, gpb * S5_STATE).astype(BF16)

    def out_map(cc):
        cc = cc.reshape(S5_BLOCKS, gpb, S5_GROUP, S5_STATE)
        return jnp.einsum("qgcp,gh->qgphc", cc, eye).reshape(S5_BLOCKS, gpb * S5_STATE, LANES).astype(BF16)

    return lbr, lbi, in_map(bbr), in_map(bbi), out_map(c_re), out_map(c_im)


def s5_mixer(dims, u_all, lam_re, lam_im, log_dt, b_re, b_im, c_re, c_im):
    assert 2 * dims.B <= SUBLANES
    lbr, lbi, wre, wim, cre, cim = _s5_maps(lam_re, lam_im, log_dt, b_re, b_im, c_re, c_im)
    rows_dir = jnp.minimum(jnp.arange(SUBLANES) // dims.B, 1)
    lam_r = lbr.reshape(2, S5_LANES)[rows_dir]
    lam_i = lbi.reshape(2, S5_LANES)[rows_dir]
    u = jnp.stack([_scan_order(dims, u_all, True, col_major=True), _scan_order(dims, u_all, False, col_major=True)])
    steps = u.shape[2]
    u = u.reshape(2 * dims.B, steps, S5_WIDTH).transpose(1, 0, 2)
    u = jnp.pad(u, ((0, 0), (0, SUBLANES - 2 * dims.B), (0, 0)))
    full = lambda shape: pl.BlockSpec(shape, lambda i: (0,) * len(shape))
    rows = S5_STEPS * SUBLANES
    y = pl.pallas_call(
        functools.partial(_s5_kernel, n_fwd_rows=dims.B),
        grid=(steps // S5_STEPS,),
        in_specs=[pl.BlockSpec((S5_STEPS, SUBLANES, S5_WIDTH), lambda i: (i, 0, 0)),
                  full((SUBLANES, S5_LANES)), full((SUBLANES, S5_LANES)),
                  full(wre.shape), full(wim.shape), full(cre.shape), full(cim.shape)],
        out_specs=pl.BlockSpec((S5_STEPS, SUBLANES, S5_WIDTH), lambda i: (i, 0, 0)),
        out_shape=jax.ShapeDtypeStruct((steps, SUBLANES, S5_WIDTH), F32),
        scratch_shapes=[pltpu.VMEM((rows, S5_LANES), F32), pltpu.VMEM((rows, S5_LANES), F32),
                        pltpu.VMEM((SUBLANES, S5_LANES), F32), pltpu.VMEM((SUBLANES, S5_LANES), F32)],
        compiler_params=_params(("arbitrary",), 32),
        name="s5_scan",
    )(u, lam_r, lam_i, wre, wim, cre, cim)

    def latent(yd, forward):
        yd = yd[dims.C:].transpose(1, 0, 2)
        if not forward:
            yd = yd[:, ::-1]
        yd = yd.reshape(dims.B, GRID_W, dims.T // GRID_W, S5_WIDTH).transpose(0, 2, 1, 3)
        return yd.reshape(dims.m_lat, S5_WIDTH)

    return latent(y[:, :dims.B], True), latent(y[:, dims.B:2 * dims.B], False)


def _gelu_tanh(x):
    return 0.5 * x * (1.0 + jnp.tanh(math.sqrt(2.0 / math.pi) * (x + 0.044715 * x * x * x)))


def _odd_out_kernel(x_ref, yf_ref, yb_ref, bf_ref, bb_ref, gl_ref, sf_ref, sb_ref, u_ref, lng, lnb, g2_ref, dsk,
                    glw, glb, e_ref, et_ref, gate_ref, w_ref, o_ref, a_ref):
    @pl.when(pl.program_id(1) == 0)
    def _():
        e = e_ref[...]
        et = et_ref[...]
        y = yf_ref[...] + yb_ref[...]
        mean = _dot_rsel(_dot_rsel(y, e) * (1.0 / RWKV_HEAD), et)
        cen = y - mean
        var = _dot_rsel(cen * cen, e) * (1.0 / RWKV_HEAD)
        yn = cen * _dot_rsel(lax.rsqrt(var + RWKV_GN_EPS), et) * lng[...] + lnb[...]
        yn = yn + bf_ref[...] + bb_ref[...]
        a_ref[:, pl.ds(0, RWKV_WIDTH)] = (yn * _dot3(_sigmoid(gl_ref[...]), g2_ref[...])).astype(BF16)
        ys = _gelu_tanh(sf_ref[...] + sb_ref[...] + dsk[...] * u_ref[...])
        a_ref[:, pl.ds(RWKV_WIDTH, S5_WIDTH)] = (ys * _sigmoid(_dot3(ys, glw[...]) + glb[...])).astype(BF16)

    m = jnp.dot(a_ref[...], w_ref[...].astype(BF16), preferred_element_type=F32)
    o_ref[...] = x_ref[...] + gate_ref[...] * m


def odd_out(dims, x_all, proj, u_all, y_f, y_b, bonus_f, bonus_b, s_f, s_b, ln_g, ln_b, g2, d_skip, glu_w, glu_b,
            w_out, gate, tn=512):
    d = x_all.shape[1]
    tm = TOKEN_BLOCK
    wd = RWKV_WIDTH
    e, et = _head_sum_matrices()
    rowblk = lambda width: pl.BlockSpec((tm, width), lambda i, j: (i, 0))
    vec = lambda width: pl.BlockSpec((1, width), lambda i, j: (0, 0))
    mat = lambda r, c: pl.BlockSpec((r, c), lambda i, j: (0, 0))
    return pl.pallas_call(
        _odd_out_kernel,
        grid=(dims.m_lat // tm, d // tn),
        in_specs=[pl.BlockSpec((tm, tn), lambda i, j: (i, j)),
                  rowblk(wd), rowblk(wd), rowblk(wd), rowblk(wd),
                  pl.BlockSpec((tm, LANES), lambda i, j: (i, ODD_GL // LANES)),
                  rowblk(S5_WIDTH), rowblk(S5_WIDTH), rowblk(S5_WIDTH),
                  vec(wd), vec(wd), mat(LANES, wd), vec(S5_WIDTH), mat(S5_WIDTH, S5_WIDTH), vec(S5_WIDTH),
                  mat(wd, LANES), mat(LANES, wd),
                  pl.BlockSpec((None, 1, tn), lambda i, j: (dims.sample_of_tile(i, tm), 0, j)),
                  pl.BlockSpec((d, tn), lambda i, j: (0, j))],
        out_specs=pl.BlockSpec((tm, tn), lambda i, j: (i, j)),
        out_shape=jax.ShapeDtypeStruct((dims.m_lat, d), F32),
        scratch_shapes=[pltpu.VMEM((tm, d), BF16)],
        compiler_params=_params(("parallel", "arbitrary"), 56),
        name="odd_out",
    )(x_all, y_f, y_b, bonus_f, bonus_b, proj, s_f, s_b, u_all, ln_g.reshape(1, wd), ln_b.reshape(1, wd), g2,
      d_skip.reshape(1, S5_WIDTH), glu_w, glu_b.reshape(1, S5_WIDTH), e, et, gate, w_out)


def odd_mixer_layer(dims, x_all, mod, norm_g, w_in, mu, w0, w2, a0, a2, g2, k_k, k_a, r_k, ln_g, ln_b,
                    lam_re, lam_im, log_dt, b_re, b_im, c_re, c_im, d_skip, glu_w, glu_b, w_out):
    w_pad = jnp.pad(w_in, ((0, 0), (0, ODD_COLS - w_in.shape[1])))
    proj = normmod_matmul(dims, x_all, norm_g, mod[0], mod[1], w_pad, tn=512)
    y_f, y_b, bonus_f, bonus_b = rwkv_mixer(dims, proj, mu, w0, w2, a0, a2, k_k, k_a, r_k)
    u_all = proj[:, ODD_S5:ODD_S5 + S5_WIDTH]
    s_f, s_b = s5_mixer(dims, u_all, lam_re, lam_im, log_dt, b_re, b_im, c_re, c_im)
    return odd_out(dims, x_all, proj, u_all, y_f, y_b, bonus_f, bonus_b, s_f, s_b, ln_g, ln_b, g2, d_skip, glu_w,
                   glu_b, w_out, mod[2])


def kernel(x, c, ctx, c_ctx, ada_w, ada_b, norm1_g, norm2_g, even_w_in, gla_gate_w, gla_gate_b, gla_norm_g, conv_w, conv_b, conv_ln_g, conv_ln_b, even_w_out, odd_w_in, rwkv_mu, rwkv_w0, rwkv_w2, rwkv_a0, rwkv_a2, rwkv_g2, rwkv_k_k, rwkv_k_a, rwkv_r_k, rwkv_ln_g, rwkv_ln_b, s5_lam_re, s5_lam_im, s5_log_dt, s5_b_re, s5_b_im, s5_c_re, s5_c_im, s5_d, s5_glu_w, s5_glu_b, odd_w_out, router_w, router_b, moe_w1, moe_b1, moe_w2, moe_b2, final_g):
    B, T, D = x.shape
    C = ctx.shape[1]
    assert ada_w.shape[0] == 2, "one even layer followed by one (final) odd layer"
    dims = Dims(B, T, C)
    cvec = jnp.zeros((SUBLANES, D), F32).at[:B].set(c).at[B].set(c_ctx)
    tab = adaln_table(cvec, ada_w, ada_b)
    x_all = jnp.concatenate([x.reshape(-1, D), ctx.reshape(-1, D)], axis=0)
    mod = tab[0].reshape(SUBLANES, N_MOD, 1, D).transpose(1, 0, 2, 3)
    x_all = even_mixer_layer(dims, x_all, mod, norm1_g[0], even_w_in[0], gla_gate_w[0], gla_gate_b[0],
                             gla_norm_g[0], conv_w[0], conv_b[0], conv_ln_g[0], conv_ln_b[0], even_w_out[0])
    x_all = moe_layer(dims, x_all, dims.m_all, mod, norm2_g[0], router_w[0], router_b[0], moe_w1[0], moe_b1[0],
                      moe_w2[0], moe_b2[0])
    mod = tab[1].reshape(SUBLANES, N_MOD, 1, D).transpose(1, 0, 2, 3)
    x_lat = odd_mixer_layer(dims, x_all, mod, norm1_g[1], odd_w_in[0], rwkv_mu[0], rwkv_w0[0], rwkv_w2[0],
                            rwkv_a0[0], rwkv_a2[0], rwkv_g2[0], rwkv_k_k[0], rwkv_k_a[0], rwkv_r_k[0],
                            rwkv_ln_g[0], rwkv_ln_b[0], s5_lam_re[0], s5_lam_im[0], s5_log_dt[0], s5_b_re[0],
                            s5_b_im[0], s5_c_re[0], s5_c_im[0], s5_d[0], s5_glu_w[0], s5_glu_b[0], odd_w_out[0])
    x_lat = moe_layer(dims, x_lat, dims.m_lat, mod, norm2_g[1], router_w[1], router_b[1], moe_w1[1], moe_b1[1],
                      moe_w2[1], moe_b2[1], final_g=final_g)
    return x_lat.reshape(B, T, D)
```

```python
import functools
import math

import jax
import jax.numpy as jnp
import numpy as np
from jax import lax
from jax.experimental import pallas as pl
from jax.experimental.pallas import tpu as pltpu

F32 = jnp.float32
BF16 = jnp.bfloat16

NORM_EPS = 1e-6
GRID_W = 64
N_MOD = 6
LANES = 128
SUBLANES = 8
VMEM_BYTES = 64 << 20

GLA_HEADS = 4
GLA_DK = 128
GLA_DV = 256
GLA_RANK = 16
GLA_NORMALIZER = 16.0
GLA_CHUNK = 64
CONV_WIDTH = 31
CONV_PAD = 16
TOKEN_BLOCK = 256

RWKV_HEAD = 64
RWKV_GN_EPS = 64e-5
S5_GROUP = 16
S5_STATE = 64
S5_MAX_RE = -1e-4

N_EXPERTS = 32
TOP_K = 4
SWIGLU_ALPHA = 1.702
SWIGLU_LIMIT = 7.0


def _params(semantics, vmem_mb):
    return pltpu.CompilerParams(dimension_semantics=semantics, vmem_limit_bytes=vmem_mb << 20)


def _bdot(a, b):
    return jnp.dot(a.astype(BF16), b.astype(BF16), preferred_element_type=F32)


def _split2(x):
    hi = x.astype(BF16)
    lo = (x - hi.astype(F32)).astype(BF16)
    return hi, lo


def _dot3(a, b):
    ah, al = _split2(a)
    bh, bl = _split2(b)
    d = functools.partial(jnp.dot, preferred_element_type=F32)
    return d(ah, bh) + d(al, bh) + d(ah, bl)


def _dot_sel(sel, b):
    b1 = b.astype(BF16)
    r1 = b - b1.astype(F32)
    b2 = r1.astype(BF16)
    b3 = (r1 - b2.astype(F32)).astype(BF16)
    d = functools.partial(jnp.dot, preferred_element_type=F32)
    return d(sel, b1) + d(sel, b2) + d(sel, b3)


def _dot_rsel(a, sel):
    a1 = a.astype(BF16)
    r1 = a - a1.astype(F32)
    a2 = r1.astype(BF16)
    a3 = (r1 - a2.astype(F32)).astype(BF16)
    d = functools.partial(jnp.dot, preferred_element_type=F32)
    return d(a1, sel) + d(a2, sel) + d(a3, sel)


def _sigmoid(x):
    return 1.0 / (1.0 + jnp.exp(-x))


def _silu(x):
    return x * _sigmoid(x)


def _log_sigmoid(x):
    return jnp.minimum(x, 0.0) - jnp.log(1.0 + jnp.exp(-jnp.abs(x)))


def _softplus(x):
    return jnp.maximum(x, 0.0) + jnp.log(1.0 + jnp.exp(-jnp.abs(x)))


class Dims:
    def __init__(self, batch, seq, ctx_len):
        self.B, self.T, self.C = batch, seq, ctx_len
        self.m_lat = batch * seq
        self.m_ctx = batch * ctx_len
        self.m_all = self.m_lat + self.m_ctx
        assert ctx_len == TOKEN_BLOCK and seq % TOKEN_BLOCK == 0 and seq % GRID_W == 0
        self.nlb = seq // TOKEN_BLOCK
        self.tm = math.gcd(math.gcd(seq, self.m_ctx), 1024)

    def sample_of_tile(self, i, tm):
        return jnp.where(i * tm < self.m_lat, (i * tm) // self.T, self.B)


def _adaln_kernel(c_ref, w_ref, b_ref, o_ref):
    o_ref[...] = _bdot(_silu(c_ref[...]), w_ref[...]) + b_ref[...]


def adaln_table(cvec, ada_w, ada_b):
    depth, d, n = ada_w.shape
    tn = 1024
    return pl.pallas_call(
        _adaln_kernel,
        grid=(depth, n // tn),
        in_specs=[
            pl.BlockSpec((SUBLANES, d), lambda l, j: (0, 0)),
            pl.BlockSpec((None, d, tn), lambda l, j: (l, 0, j)),
            pl.BlockSpec((None, 1, tn), lambda l, j: (l, 0, j)),
        ],
        out_specs=pl.BlockSpec((None, SUBLANES, tn), lambda l, j: (l, 0, j)),
        out_shape=jax.ShapeDtypeStruct((depth, SUBLANES, n), F32),
        compiler_params=_params(("parallel", "parallel"), 40),
        name="adaln_table",
    )(cvec, ada_w, ada_b.reshape(depth, 1, n))


def _normmod(x, g, shift, scale):
    y = x * lax.rsqrt(jnp.mean(x * x, axis=-1, keepdims=True) + NORM_EPS) * g
    return y * (1.0 + scale) + shift


def _normmod_mm_kernel(x_ref, g_ref, sh_ref, sc_ref, w_ref, o_ref, h_ref):
    @pl.when(pl.program_id(1) == 0)
    def _():
        h_ref[...] = _normmod(x_ref[...], g_ref[...], sh_ref[...], sc_ref[...]).astype(BF16)

    o_ref[...] = jnp.dot(h_ref[...], w_ref[...].astype(BF16), preferred_element_type=F32)


def normmod_matmul(dims, x_all, g, shift, scale, w, tn):
    m, d = x_all.shape
    n = w.shape[1]
    tm = dims.tm
    samp = lambda i, j: (dims.sample_of_tile(i, tm), 0, 0)
    return pl.pallas_call(
        _normmod_mm_kernel,
        grid=(m // tm, n // tn),
        in_specs=[
            pl.BlockSpec((tm, d), lambda i, j: (i, 0)),
            pl.BlockSpec((1, d), lambda i, j: (0, 0)),
            pl.BlockSpec((None, 1, d), samp),
            pl.BlockSpec((None, 1, d), samp),
            pl.BlockSpec((d, tn), lambda i, j: (0, j)),
        ],
        out_specs=pl.BlockSpec((tm, tn), lambda i, j: (i, j)),
        out_shape=jax.ShapeDtypeStruct((m, n), F32),
        scratch_shapes=[pltpu.VMEM((tm, d), BF16)],
        compiler_params=_params(("parallel", "arbitrary"), 56),
        name="normmod_matmul",
    )(x_all, g.reshape(1, d), shift, scale, w)


EVEN_Q, EVEN_K, EVEN_V, EVEN_OG, EVEN_CA, EVEN_CB, EVEN_LRF, EVEN_LRB = 0, 512, 1024, 2048, 3072, 4096, 5120, 5248
EVEN_COLS = 5376


def _gla_chunk(q, k, v, lr, gw, gb, st_ref, forward):
    length = q.shape[0]
    row = lax.broadcasted_iota(jnp.int32, (length, length), 0)
    col = lax.broadcasted_iota(jnp.int32, (length, length), 1)
    tri = (col <= row) if forward else (col >= row)
    g = _log_sigmoid(_dot3(lr, gw) + gb) * (1.0 / GLA_NORMALIZER)
    cum = _dot_sel(tri.astype(BF16), g)
    tot = cum[length - 1:length, :] if forward else cum[0:1, :]
    q_dec = (q * (GLA_DK ** -0.5) * jnp.exp(cum)).astype(BF16)
    k_inv = (k * jnp.exp(-cum)).astype(BF16)
    k_st = (k * jnp.exp(tot - cum)).astype(BF16)
    scores = lax.dot_general(q_dec, k_inv, (((1,), (1,)), ((), ())), preferred_element_type=F32)
    scores = jnp.where(tri, scores, 0.0)
    st = st_ref[...]
    o = _bdot(scores, v) + lax.dot_general(q_dec, st.astype(BF16), (((1,), (1,)), ((), ())),
                                           preferred_element_type=F32)
    st_ref[...] = st * jnp.exp(tot) + lax.dot_general(
        v.astype(BF16), k_st, (((0,), (0,)), ((), ())), preferred_element_type=F32)
    return o


def _gla_kernel(qf, kf, vf, lf, qb, kb, vb, lb, gwf, gbf, gwb, gbb, of, ob, sf, sb):
    @pl.when(pl.program_id(2) == 0)
    def _():
        sf[...] = jnp.zeros_like(sf)
        sb[...] = jnp.zeros_like(sb)

    nchunk = TOKEN_BLOCK // GLA_CHUNK
    for c in range(nchunk):
        r = pl.ds(c * GLA_CHUNK, GLA_CHUNK)
        of[r, :] = _gla_chunk(qf[r, :], kf[r, :], vf[r, :], lf[r, :], gwf[...], gbf[...], sf, True)
        r = pl.ds((nchunk - 1 - c) * GLA_CHUNK, GLA_CHUNK)
        ob[r, :] = _gla_chunk(qb[r, :], kb[r, :], vb[r, :], lb[r, :], gwb[...], gbb[...], sb, False)


def gla_scan(dims, proj, gate_w_pad, gate_b):
    B, nlb = dims.B, dims.nlb
    blk = TOKEN_BLOCK

    def fblk(b, s):
        return jnp.where(s == 0, B * nlb + b, b * nlb + s - 1)

    def bblk(b, s):
        return jnp.where(s == 0, B * nlb + b, b * nlb + nlb - s)

    def col(blk_fn, width, start):
        return pl.BlockSpec((blk, width), lambda b, h, s: (blk_fn(b, s), start // width + h))

    def lr(blk_fn, start):
        return pl.BlockSpec((blk, LANES), lambda b, h, s: (blk_fn(b, s), start // LANES))

    def gw(d):
        return pl.BlockSpec((None, LANES, GLA_DK), lambda b, h, s: (d, 0, h))

    def gb(d):
        return pl.BlockSpec((None, 1, GLA_DK), lambda b, h, s: (d, 0, h))

    out = jax.ShapeDtypeStruct((dims.m_all, GLA_HEADS * GLA_DV), F32)
    return pl.pallas_call(
        _gla_kernel,
        grid=(B, GLA_HEADS, nlb + 1),
        in_specs=[
            col(fblk, GLA_DK, EVEN_Q), col(fblk, GLA_DK, EVEN_K), col(fblk, GLA_DV, EVEN_V), lr(fblk, EVEN_LRF),
            col(bblk, GLA_DK, EVEN_Q), col(bblk, GLA_DK, EVEN_K), col(bblk, GLA_DV, EVEN_V), lr(bblk, EVEN_LRB),
            gw(0), gb(0), gw(1), gb(1),
        ],
        out_specs=[
            pl.BlockSpec((blk, GLA_DV), lambda b, h, s: (fblk(b, s), h)),
            pl.BlockSpec((blk, GLA_DV), lambda b, h, s: (bblk(b, s), h)),
        ],
        out_shape=[out, out],
        scratch_shapes=[pltpu.VMEM((GLA_DV, GLA_DK), F32), pltpu.VMEM((GLA_DV, GLA_DK), F32)],
        compiler_params=_params(("parallel", "parallel", "arbitrary"), 32),
        name="gla_scan",
    )(proj, proj, proj, proj, proj, proj, proj, proj, gate_w_pad, gate_b, gate_w_pad, gate_b)


def _conv_kernel(ca_ref, cb_ref, w_ref, b_ref, g_ref, beta_ref, o_ref, pad_ref, *, row_len):
    blk, ch = o_ref.shape
    seg = row_len + 2 * CONV_PAD
    nrows = blk // row_len
    h = ca_ref[...] * _sigmoid(cb_ref[...])
    for r in range(nrows):
        pad_ref[pl.ds(r * seg, CONV_PAD), :] = jnp.zeros((CONV_PAD, ch), F32)
        pad_ref[pl.ds(r * seg + CONV_PAD, row_len), :] = h[r * row_len:(r + 1) * row_len, :]
        pad_ref[pl.ds(r * seg + CONV_PAD + row_len, CONV_PAD), :] = jnp.zeros((CONV_PAD, ch), F32)
    half = CONV_WIDTH // 2
    sub = 64
    cw = 256
    for r in range(nrows):
        for m in range(row_len // sub):
            for c in range(ch // cw):
                cs = pl.ds(c * cw, cw)
                acc = jnp.zeros((sub, cw), F32)
                base = r * seg + CONV_PAD + m * sub - half
                for j in range(CONV_WIDTH):
                    acc = acc + w_ref[pl.ds(j, 1), cs] * pad_ref[pl.ds(base + j, sub), cs]
                o_ref[pl.ds(r * row_len + m * sub, sub), cs] = acc
    hf = o_ref[...] + b_ref[...]
    mean = jnp.mean(hf, axis=-1, keepdims=True)
    cen = hf - mean
    var = jnp.mean(cen * cen, axis=-1, keepdims=True)
    o_ref[...] = _silu(cen * lax.rsqrt(var + NORM_EPS) * g_ref[...] + beta_ref[...])


def conformer_conv(dims, proj, conv_w, conv_b, ln_g, ln_b):
    ch = conv_w.shape[1]
    blk = TOKEN_BLOCK
    n_lat = dims.m_lat // blk

    def run(row_len, nblk, blk0):
        vec = lambda: pl.BlockSpec((1, ch), lambda i: (0, 0))
        return pl.pallas_call(
            functools.partial(_conv_kernel, row_len=row_len),
            grid=(nblk,),
            in_specs=[
                pl.BlockSpec((blk, ch), lambda i: (blk0 + i, EVEN_CA // ch)),
                pl.BlockSpec((blk, ch), lambda i: (blk0 + i, EVEN_CB // ch)),
                pl.BlockSpec((CONV_WIDTH, ch), lambda i: (0, 0)),
                vec(), vec(), vec(),
            ],
            out_specs=pl.BlockSpec((blk, ch), lambda i: (i, 0)),
            out_shape=jax.ShapeDtypeStruct((nblk * blk, ch), F32),
            scratch_shapes=[pltpu.VMEM((blk // row_len * (row_len + 2 * CONV_PAD), ch), F32)],
            compiler_params=_params(("parallel",), 32),
            name="conformer_conv",
        )(proj, proj, conv_w, conv_b.reshape(1, ch), ln_g.reshape(1, ch), ln_b.reshape(1, ch))

    lat = run(GRID_W, n_lat, 0)
    ctx = run(dims.C, dims.m_ctx // blk, n_lat)
    return jnp.concatenate([lat, ctx], axis=0)


def _even_out_kernel(x_ref, of_ref, ob_ref, og_ref, cv_ref, ng_ref, gate_ref, w_ref, o_ref, a_ref):
    @pl.when(pl.program_id(1) == 0)
    def _():
        for h in range(GLA_HEADS):
            cs = pl.ds(h * GLA_DV, GLA_DV)
            o = of_ref[:, cs] + ob_ref[:, cs]
            y = o * lax.rsqrt(jnp.mean(o * o, axis=-1, keepdims=True) + NORM_EPS) * ng_ref[...]
            a_ref[:, cs] = (y * _silu(og_ref[:, cs])).astype(BF16)
        nv = GLA_HEADS * GLA_DV
        a_ref[:, pl.ds(nv, cv_ref.shape[1])] = cv_ref[...].astype(BF16)

    m = jnp.dot(a_ref[...], w_ref[...].astype(BF16), preferred_element_type=F32)
    o_ref[...] = x_ref[...] + gate_ref[...] * m


def even_out(dims, x_all, proj, o_f, o_b, conv_out, gla_norm_g, w_out, gate, tn=512):
    m, d = x_all.shape
    tm = dims.tm
    nv = GLA_HEADS * GLA_DV
    return pl.pallas_call(
        _even_out_kernel,
        grid=(m // tm, d // tn),
        in_specs=[
            pl.BlockSpec((tm, tn), lambda i, j: (i, j)),
            pl.BlockSpec((tm, nv), lambda i, j: (i, 0)),
            pl.BlockSpec((tm, nv), lambda i, j: (i, 0)),
            pl.BlockSpec((tm, nv), lambda i, j: (i, EVEN_OG // nv)),
            pl.BlockSpec((tm, d - nv), lambda i, j: (i, 0)),
            pl.BlockSpec((1, GLA_DV), lambda i, j: (0, 0)),
            pl.BlockSpec((None, 1, tn), lambda i, j: (dims.sample_of_tile(i, tm), 0, j)),
            pl.BlockSpec((d, tn), lambda i, j: (0, j)),
        ],
        out_specs=pl.BlockSpec((tm, tn), lambda i, j: (i, j)),
        out_shape=jax.ShapeDtypeStruct((m, d), F32),
        scratch_shapes=[pltpu.VMEM((tm, d), BF16)],
        compiler_params=_params(("parallel", "arbitrary"), 56),
        name="even_out",
    )(x_all, o_f, o_b, proj, conv_out, gla_norm_g.reshape(1, GLA_DV), gate, w_out)


def even_weights(w_in, gate_w):
    d = w_in.shape[0]
    kw, vw = GLA_HEADS * GLA_DK, GLA_HEADS * GLA_DV
    main = 2 * kw + 2 * vw
    lr_f = w_in[:, main:main + GLA_RANK]
    lr_b = w_in[:, main + GLA_RANK:main + 2 * GLA_RANK]
    zpad = jnp.zeros((d, LANES - GLA_RANK), F32)
    w_pad = jnp.concatenate([w_in[:, :main], w_in[:, main + 2 * GLA_RANK:], lr_f, zpad, lr_b, zpad], axis=1)
    assert w_pad.shape[1] == EVEN_COLS
    gw_pad = jnp.pad(gate_w, ((0, 0), (0, LANES - GLA_RANK), (0, 0)))
    return w_pad, gw_pad


def even_mixer_layer(dims, x_all, mod, norm_g, w_in, gate_w, gate_b, gla_norm_g, conv_w, conv_b, conv_ln_g,
                     conv_ln_b, w_out):
    w_pad, gw_pad = even_weights(w_in, gate_w)
    proj = normmod_matmul(dims, x_all, norm_g, mod[0], mod[1], w_pad, tn=768)
    o_f, o_b = gla_scan(dims, proj, gw_pad, gate_b.reshape(2, 1, -1))
    conv_out = conformer_conv(dims, proj, conv_w, conv_b, conv_ln_g, conv_ln_b)
    return even_out(dims, x_all, proj, o_f, o_b, conv_out, gla_norm_g, w_out, mod[2])


EXPERT_TILE = 512
UP_CHUNK = 1024
DOWN_CHUNK = 1024
ROW_TILE = 256
NEG_INF = float("-inf")


def _router_kernel(x_ref, g_ref, sh_ref, sc_ref, rw_ref, rb_ref, f_ref, ti_ref, tw_ref, pos_ref, cnt_ref, carry_ref):
    @pl.when(pl.program_id(0) == 0)
    def _():
        carry_ref[...] = jnp.zeros_like(carry_ref)

    f = _normmod(x_ref[...], g_ref[...], sh_ref[...], sc_ref[...])
    f_ref[...] = f
    tm = f.shape[0]
    work = _dot3(f, rw_ref[...]) + rb_ref[...]
    lane = lax.broadcasted_iota(jnp.int32, (tm, LANES), 1)
    vals, idxs = [], []
    for _ in range(TOP_K):
        m = jnp.max(work, axis=-1, keepdims=True)
        idx = jnp.min(jnp.where(work == m, lane, LANES), axis=-1, keepdims=True)
        vals.append(m)
        idxs.append(idx)
        work = jnp.where(lane == idx, NEG_INF, work)
    es = [jnp.exp(v - vals[0]) for v in vals]
    denom = es[0] + es[1] + es[2] + es[3]
    sel = (work == NEG_INF).astype(F32)
    row = lax.broadcasted_iota(jnp.int32, (tm, tm), 0)
    col = lax.broadcasted_iota(jnp.int32, (tm, tm), 1)
    before = jnp.dot((col < row).astype(BF16), sel.astype(BF16), preferred_element_type=F32)
    rank = before + carry_ref[...]
    ti = jnp.zeros((tm, LANES), jnp.int32)
    tw = jnp.zeros((tm, LANES), F32)
    pos = jnp.zeros((tm, LANES), jnp.int32)
    for k in range(TOP_K):
        pk = jnp.sum(jnp.where(lane == idxs[k], rank, 0.0), axis=-1, keepdims=True).astype(jnp.int32)
        ti = jnp.where(lane == k, idxs[k], ti)
        tw = jnp.where(lane == k, es[k] / denom, tw)
        pos = jnp.where(lane == k, pk, pos)
    ti_ref[...] = ti
    tw_ref[...] = tw
    pos_ref[...] = pos
    carry_ref[...] = carry_ref[...] + jnp.sum(sel, axis=0, keepdims=True)
    cnt_ref[...] = carry_ref[...]


def moe_router(dims, x_all, m_rows, g, shift, scale, router_w, router_b):
    d = x_all.shape[1]
    tm = min(dims.tm, 512)
    rw = jnp.pad(router_w, ((0, 0), (0, LANES - N_EXPERTS)))
    rb = jnp.concatenate([router_b, jnp.full((LANES - N_EXPERTS,), -1e30, F32)]).reshape(1, LANES)
    samp = lambda i: (dims.sample_of_tile(i, tm), 0, 0)
    lane_out = lambda dt: jax.ShapeDtypeStruct((m_rows, LANES), dt)
    lane_spec = lambda: pl.BlockSpec((tm, LANES), lambda i: (i, 0))
    return pl.pallas_call(
        _router_kernel,
        grid=(m_rows // tm,),
        in_specs=[
            pl.BlockSpec((tm, d), lambda i: (i, 0)),
            pl.BlockSpec((1, d), lambda i: (0, 0)),
            pl.BlockSpec((None, 1, d), samp),
            pl.BlockSpec((None, 1, d), samp),
            pl.BlockSpec((d, LANES), lambda i: (0, 0)),
            pl.BlockSpec((1, LANES), lambda i: (0, 0)),
        ],
        out_specs=[pl.BlockSpec((tm, d), lambda i: (i, 0)), lane_spec(), lane_spec(), lane_spec(),
                   pl.BlockSpec((1, LANES), lambda i: (0, 0))],
        out_shape=[jax.ShapeDtypeStruct((m_rows, d), F32), lane_out(jnp.int32), lane_out(F32), lane_out(jnp.int32),
                   jax.ShapeDtypeStruct((1, LANES), F32)],
        scratch_shapes=[pltpu.VMEM((1, LANES), F32)],
        compiler_params=_params(("arbitrary",), 40),
        name="moe_router",
    )(x_all, g.reshape(1, d), shift, scale, rw, rb)


def _dispatch_kernel(slot_ref, f_ref, xs_in, xs_hbm, sem):
    del xs_in

    def copy(r, k):
        return pltpu.make_async_copy(f_ref.at[pl.ds(r, 1)], xs_hbm.at[pl.ds(slot_ref[r * TOP_K + k], 1)], sem)

    def issue(r, carry):
        for k in range(TOP_K):
            copy(r, k).start()
        return carry

    def drain(r, carry):
        for k in range(TOP_K):
            copy(r, k).wait()
        return carry

    lax.fori_loop(0, ROW_TILE, issue, 0)
    lax.fori_loop(0, ROW_TILE, drain, 0)


def moe_dispatch(f, slot_flat, n_slots):
    m, d = f.shape
    xs0 = jnp.zeros((n_slots, d), F32)
    return pl.pallas_call(
        _dispatch_kernel,
        grid=(m // ROW_TILE,),
        in_specs=[
            pl.BlockSpec((ROW_TILE * TOP_K,), lambda i: (i,), memory_space=pltpu.SMEM),
            pl.BlockSpec((ROW_TILE, d), lambda i: (i, 0)),
            pl.BlockSpec(memory_space=pl.ANY),
        ],
        out_specs=pl.BlockSpec(memory_space=pl.ANY),
        out_shape=jax.ShapeDtypeStruct((n_slots, d), F32),
        scratch_shapes=[pltpu.SemaphoreType.DMA(())],
        input_output_aliases={2: 0},
        compiler_params=_params(("arbitrary",), 16),
        name="moe_dispatch",
    )(slot_flat, f, xs0)


def _expert_changed(te_ref):
    t = pl.program_id(1)
    return jnp.logical_or(t == 0, te_ref[t] != te_ref[jnp.maximum(t - 1, 0)])


def _expert_up_kernel(te_ref, tv_ref, x_ref, w_ref, b_ref, h_ref, wb_ref):
    half = UP_CHUNK // 2

    @pl.when(_expert_changed(te_ref))
    def _():
        wb_ref[...] = w_ref[...].astype(BF16)

    @pl.when(tv_ref[pl.program_id(1)] != 0)
    def _():
        slab = 128
        even = lax.broadcasted_iota(jnp.int32, (slab, half), 1) % 2 == 0

        def act(hh):
            glu = jnp.minimum(hh, SWIGLU_LIMIT)
            lin = jnp.clip(hh, -SWIGLU_LIMIT, SWIGLU_LIMIT) + 1.0
            return glu * _sigmoid(SWIGLU_ALPHA * glu) * pltpu.roll(lin, half - 1, axis=1)

        for r in range(x_ref.shape[0] // slab):
            rows = pl.ds(r * slab, slab)
            h = jnp.dot(x_ref[rows, :].astype(BF16), wb_ref[...], preferred_element_type=F32) + b_ref[...]
            packed = jnp.where(even, act(h[:, :half]), pltpu.roll(act(h[:, half:]), 1, axis=1))
            h_ref[rows, :] = packed.astype(BF16)

    @pl.when(tv_ref[pl.program_id(1)] == 0)
    def _():
        h_ref[...] = jnp.zeros_like(h_ref)


def _expert_down_kernel(te_ref, tv_ref, h_ref, w_ref, b_ref, y_ref, stage_ref, wb_ref):
    half = UP_CHUNK // 4

    @pl.when(_expert_changed(te_ref))
    def _():
        for c in range(stage_ref.shape[0]):
            cols = pl.ds(c * LANES, LANES)
            for blk in range(w_ref.shape[0] // (2 * half)):
                for s in range(2):
                    stage_ref[c, pl.ds(blk * 2 * half + s, half, stride=2), :] = w_ref[
                        pl.ds((2 * blk + s) * half, half), cols]
            wb_ref[:, cols] = stage_ref[c].astype(BF16)

    @pl.when(tv_ref[pl.program_id(1)] != 0)
    def _():
        y_ref[...] = jnp.dot(h_ref[...], wb_ref[...], preferred_element_type=F32) + b_ref[...]

    @pl.when(tv_ref[pl.program_id(1)] == 0)
    def _():
        y_ref[...] = jnp.zeros_like(y_ref)


def moe_experts(xs, tile_expert, tile_valid, layer, w1, b1, w2, b2):
    n_slots, d = xs.shape
    _, n_exp, _, two_f = w1.shape
    f = two_f // 2
    nt = n_slots // EXPERT_TILE
    grid_spec = pltpu.PrefetchScalarGridSpec(
        num_scalar_prefetch=2,
        grid=(two_f // UP_CHUNK, nt),
        in_specs=[
            pl.BlockSpec((EXPERT_TILE, d), lambda j, t, te, tv: (t, 0)),
            pl.BlockSpec((None, None, d, UP_CHUNK), lambda j, t, te, tv: (layer, te[t], 0, j)),
            pl.BlockSpec((None, 1, UP_CHUNK), lambda j, t, te, tv: (te[t], 0, j)),
        ],
        out_specs=pl.BlockSpec((EXPERT_TILE, UP_CHUNK // 2), lambda j, t, te, tv: (t, j)),
        scratch_shapes=[pltpu.VMEM((d, UP_CHUNK), BF16)],
    )
    hidden = pl.pallas_call(
        _expert_up_kernel,
        grid_spec=grid_spec,
        out_shape=jax.ShapeDtypeStruct((n_slots, f), BF16),
        compiler_params=_params(("arbitrary", "arbitrary"), 48),
        name="moe_expert_up",
    )(tile_expert, tile_valid, xs, w1, b1.reshape(n_exp, 1, two_f))
    grid_spec = pltpu.PrefetchScalarGridSpec(
        num_scalar_prefetch=2,
        grid=(d // DOWN_CHUNK, nt),
        in_specs=[
            pl.BlockSpec((EXPERT_TILE, f), lambda j, t, te, tv: (t, 0)),
            pl.BlockSpec((None, None, f, DOWN_CHUNK), lambda j, t, te, tv: (layer, te[t], 0, j)),
            pl.BlockSpec((None, 1, DOWN_CHUNK), lambda j, t, te, tv: (te[t], 0, j)),
        ],
        out_specs=pl.BlockSpec((EXPERT_TILE, DOWN_CHUNK), lambda j, t, te, tv: (t, j)),
        scratch_shapes=[pltpu.VMEM((DOWN_CHUNK // LANES, f, LANES), F32), pltpu.VMEM((f, DOWN_CHUNK), BF16)],
    )
    return pl.pallas_call(
        _expert_down_kernel,
        grid_spec=grid_spec,
        out_shape=jax.ShapeDtypeStruct((n_slots, d), F32),
        compiler_params=_params(("arbitrary", "arbitrary"), 56),
        name="moe_expert_down",
    )(tile_expert, tile_valid, hidden, w2, b2.reshape(n_exp, 1, d))


def _combine_kernel(slot_ref, x_ref, tw_ref, gate_ref, fg_ref, ys_hbm, o_ref, buf, sem, *, final_norm):
    def copy(r, k):
        return pltpu.make_async_copy(ys_hbm.at[pl.ds(slot_ref[r * TOP_K + k], 1)], buf.at[k, pl.ds(r, 1)], sem)

    def issue(r, carry):
        for k in range(TOP_K):
            copy(r, k).start()
        return carry

    def drain(r, carry):
        for k in range(TOP_K):
            copy(r, k).wait()
        return carry

    lax.fori_loop(0, ROW_TILE, issue, 0)
    lax.fori_loop(0, ROW_TILE, drain, 0)
    tw = tw_ref[...]
    y = tw[:, 0:1] * buf[0]
    for k in range(1, TOP_K):
        y = y + tw[:, k:k + 1] * buf[k]
    out = x_ref[...] + gate_ref[...] * y
    if final_norm:
        out = out * lax.rsqrt(jnp.mean(out * out, axis=-1, keepdims=True) + NORM_EPS) * fg_ref[...]
    o_ref[...] = out


def moe_combine(dims, x_all, m_rows, slot_flat, top_w, gate, ys, final_g):
    d = x_all.shape[1]
    tm = ROW_TILE
    fg = jnp.ones((1, d), F32) if final_g is None else final_g.reshape(1, d)
    return pl.pallas_call(
        functools.partial(_combine_kernel, final_norm=final_g is not None),
        grid=(m_rows // tm,),
        in_specs=[
            pl.BlockSpec((tm * TOP_K,), lambda i: (i,), memory_space=pltpu.SMEM),
            pl.BlockSpec((tm, d), lambda i: (i, 0)),
            pl.BlockSpec((tm, LANES), lambda i: (i, 0)),
            pl.BlockSpec((None, 1, d), lambda i: (dims.sample_of_tile(i, tm), 0, 0)),
            pl.BlockSpec((1, d), lambda i: (0, 0)),
            pl.BlockSpec(memory_space=pl.ANY),
        ],
        out_specs=pl.BlockSpec((tm, d), lambda i: (i, 0)),
        out_shape=jax.ShapeDtypeStruct((m_rows, d), F32),
        scratch_shapes=[pltpu.VMEM((TOP_K, tm, d), F32), pltpu.SemaphoreType.DMA(())],
        compiler_params=_params(("arbitrary",), 40),
        name="moe_combine",
    )(slot_flat, x_all, top_w, gate, fg, ys)


def moe_layer(dims, x_all, m_rows, mod, norm_g, router_w, router_b, layer, w1, b1, w2, b2, final_g=None):
    n_exp = w2.shape[1]
    f_tok, top_i, top_w, pos, cnt = moe_router(dims, x_all, m_rows, norm_g, mod[3], mod[4], router_w, router_b)
    counts = cnt[0, :n_exp].astype(jnp.int32)
    tiles_e = (counts + EXPERT_TILE - 1) // EXPERT_TILE
    tile_end = jnp.cumsum(tiles_e)
    offset = (tile_end - tiles_e) * EXPERT_TILE
    slot = (offset[top_i[:, :TOP_K]] + pos[:, :TOP_K]).reshape(-1)
    nt = m_rows * TOP_K // EXPERT_TILE + n_exp
    tile_ids = jnp.arange(nt, dtype=jnp.int32)
    tile_expert = jnp.minimum(jnp.sum(tile_ids[:, None] >= tile_end[None, :], axis=1), n_exp - 1).astype(jnp.int32)
    tile_valid = (tile_ids < tile_end[-1]).astype(jnp.int32)
    xs = moe_dispatch(f_tok, slot, nt * EXPERT_TILE)
    ys = moe_experts(xs, tile_expert, tile_valid, layer, w1, b1, w2, b2)
    return moe_combine(dims, x_all, m_rows, slot, top_w, mod[5], ys, final_g)


ODD_R, ODD_K, ODD_V, ODD_WL, ODD_AL, ODD_GL, ODD_S5 = 0, 1536, 3072, 4608, 4736, 4864, 4992
ODD_COLS = 5632
RWKV_WIDTH = 1536
RWKV_HEADS = RWKV_WIDTH // RWKV_HEAD
RWKV_RANK = 64
RWKV_GROUPS = 3
RWKV_GROUP_HEADS = RWKV_HEADS // RWKV_GROUPS
RWKV_KFOLD = RWKV_HEAD // 2
SCAN_STEPS = 8


def _head_sum_matrices():
    ch = np.arange(RWKV_WIDTH)[:, None] // RWKV_HEAD
    e = (ch == np.arange(LANES)[None, :]).astype(np.float32)
    return jnp.asarray(e, BF16), jnp.asarray(e.T, BF16)


def _rwkv_prep_kernel(r_ref, k_ref, v_ref, wl_ref, al_ref, rn_ref, kn_ref, vn_ref, wln_ref, aln_ref,
                      mur, muk, muv, muwl, mual, w0, w2, a0, a2, kk_w, ka_w, rk_w, e_ref, et_ref,
                      dec_o, kk_o, b_o, k_o, wr_o, v_o, bonus_o, br_o, kr_o, *, forward, nlb, n_lat_blocks):
    i = pl.program_id(0)
    tm = r_ref.shape[0]
    is_lat = i < n_lat_blocks
    if forward:
        edge = jnp.logical_or(jnp.logical_not(is_lat), i % nlb == 0)
    else:
        edge = jnp.logical_or(jnp.logical_not(is_lat), i % nlb == nlb - 1)
    keep = jnp.where(edge, 0.0, 1.0)

    def mix(x_ref, n_ref, mu_ref):
        x = x_ref[...]
        rows = lax.broadcasted_iota(jnp.int32, x.shape, 0)
        if forward:
            sh = jnp.where(rows == 0, n_ref[SUBLANES - 1:SUBLANES, :] * keep, pltpu.roll(x, 1, axis=0))
        else:
            sh = jnp.where(rows == tm - 1, n_ref[0:1, :] * keep, pltpu.roll(x, tm - 1, axis=0))
        return x + (sh - x) * mu_ref[...]

    e = e_ref[...]
    et = et_ref[...]
    r = mix(r_ref, rn_ref, mur)
    k = mix(k_ref, kn_ref, muk)
    v = mix(v_ref, vn_ref, muv)
    wl = mix(wl_ref, wln_ref, muwl)
    al = mix(al_ref, aln_ref, mual)
    w = -_softplus(-(w0[...] + _dot3(jnp.tanh(wl), w2[...]))) - 0.5
    decay = jnp.exp(-jnp.exp(w))
    a = _sigmoid(a0[...] + _dot3(al, a2[...]))
    kk = k * kk_w[...]
    inv = 1.0 / jnp.maximum(jnp.sqrt(_dot_rsel(kk * kk, e)), 1e-12)
    kk = kk * _dot_rsel(inv, et)
    k = k * (1.0 + (a - 1.0) * ka_w[...])
    b = kk * a
    if forward:
        order = lambda x: x
    else:
        rows = lax.broadcasted_iota(jnp.int32, (tm, tm), 0)
        cols = lax.broadcasted_iota(jnp.int32, (tm, tm), 1)
        flip = (rows + cols == tm - 1).astype(BF16)
        order = lambda x: _dot_sel(flip, x)
    dec_o[...] = order(decay)
    kk_o[...] = order(kk)
    b_o[...] = order(b)
    k_o[...] = order(k)
    wr_o[...] = order(decay * r)
    v_o[...] = order(v)
    br_o[...] = order(_dot_rsel(b * r, e))
    kr_o[...] = order(_dot_rsel(k * r, e))
    bonus_o[...] = _dot_rsel(_dot_rsel(r * k * rk_w[...], e), et) * v


def rwkv_prep(dims, proj, direction, mu, w0, w2, a0, a2, k_k, k_a, r_k):
    wd = RWKV_WIDTH
    tm = TOKEN_BLOCK
    nblk = dims.m_all // tm
    forward = direction == 0
    per8 = tm // SUBLANES
    last8 = dims.m_all // SUBLANES - 1

    def nb(i):
        return jnp.maximum(i * per8 - 1, 0) if forward else jnp.minimum((i + 1) * per8, last8)

    def main(width, start):
        return pl.BlockSpec((tm, width), lambda i: (i, start // width))

    def neigh(width, start):
        return pl.BlockSpec((SUBLANES, width), lambda i: (nb(i), start // width))

    vec = lambda width: pl.BlockSpec((1, width), lambda i: (0, 0))
    lo = direction * RWKV_RANK
    lane_pad = lambda x: jnp.zeros((LANES,) + x.shape[1:], F32).at[lo:lo + RWKV_RANK].set(x)
    mu_d = mu[direction]
    e, et = _head_sum_matrices()
    n_lat = dims.m_lat // tm
    steps = dims.C + dims.T

    def seq_pos(i):
        is_lat = i < n_lat
        sample = jnp.where(is_lat, i // dims.nlb, i - n_lat)
        if forward:
            return sample, jnp.where(is_lat, 1 + i % dims.nlb, 0), 0
        return sample, jnp.where(is_lat, dims.nlb - i % dims.nlb, 0), 0

    seq_spec = lambda width: pl.BlockSpec((None, tm, width), seq_pos)
    wide = jax.ShapeDtypeStruct((dims.B, steps, wd), F32)
    lane = jax.ShapeDtypeStruct((dims.B, steps, LANES), F32)
    return pl.pallas_call(
        functools.partial(_rwkv_prep_kernel, forward=forward, nlb=dims.nlb, n_lat_blocks=dims.m_lat // tm),
        grid=(nblk,),
        in_specs=[main(wd, ODD_R), main(wd, ODD_K), main(wd, ODD_V), main(LANES, ODD_WL), main(LANES, ODD_AL),
                  neigh(wd, ODD_R), neigh(wd, ODD_K), neigh(wd, ODD_V), neigh(LANES, ODD_WL), neigh(LANES, ODD_AL),
                  vec(wd), vec(wd), vec(wd), vec(LANES), vec(LANES),
                  vec(wd), pl.BlockSpec((LANES, wd), lambda i: (0, 0)),
                  vec(wd), pl.BlockSpec((LANES, wd), lambda i: (0, 0)),
                  vec(wd), vec(wd), vec(wd),
                  pl.BlockSpec((wd, LANES), lambda i: (0, 0)), pl.BlockSpec((LANES, wd), lambda i: (0, 0))],
        out_specs=[seq_spec(wd)] * 6 + [pl.BlockSpec((tm, wd), lambda i: (i, 0))] + [seq_spec(LANES)] * 2,
        out_shape=[wide] * 6 + [jax.ShapeDtypeStruct((dims.m_all, wd), F32)] + [lane] * 2,
        compiler_params=_params(("parallel",), 56),
        name="rwkv_prep",
    )(proj, proj, proj, proj, proj, proj, proj, proj, proj, proj,
      mu_d[0:wd].reshape(1, wd), mu_d[wd:2 * wd].reshape(1, wd), mu_d[2 * wd:3 * wd].reshape(1, wd),
      lane_pad(mu_d[3 * wd:3 * wd + RWKV_RANK]).reshape(1, LANES),
      lane_pad(mu_d[3 * wd + RWKV_RANK:3 * wd + 2 * RWKV_RANK]).reshape(1, LANES),
      w0[direction].reshape(1, wd), lane_pad(w2[direction]), a0[direction].reshape(1, wd), lane_pad(a2[direction]),
      k_k.reshape(1, wd), k_a.reshape(1, wd), r_k.reshape(1, wd), e, et)


def _rwkv_scan_kernel(*refs):
    feats = refs[0:5]
    v_ref, br_ref, kr_ref, y_ref, st_ref, acc_ref = refs[5:11]

    @pl.when(pl.program_id(0) == 0)
    def _():
        st_ref[...] = jnp.zeros_like(st_ref)

    ngroups, nk, nv, gw = st_ref.shape
    cg = gw // 2

    def step(t, carry):
        def rows8(n, g, kb):
            return feats[n][t, pl.ds(pl.multiple_of(kb * SUBLANES, SUBLANES), SUBLANES), pl.ds(g * gw, gw)]

        def row(blk, j):
            return jnp.broadcast_to(blk[j:j + 1, :], (nv, gw))

        for g in range(ngroups):
            def reduce_k(kb, acc, g=g):
                sa, z = acc
                kk8, wr8 = rows8(1, g, kb), rows8(4, g, kb)
                for j in range(SUBLANES):
                    s = st_ref[g, kb * SUBLANES + j]
                    sa = sa - s * row(kk8, j)
                    z = z + s * row(wr8, j)
                return sa, z

            zero = jnp.zeros((nv, gw), F32)
            sa, z = lax.fori_loop(0, nk // SUBLANES, reduce_k, (zero, zero))
            acc_ref[g, 0] = sa
            acc_ref[g, 1] = z

        for g in range(ngroups):
            lanes = pl.ds(g * gw, gw)
            sa = acc_ref[g, 0]
            z = acc_ref[g, 1]
            sa = sa + pltpu.roll(sa, cg, axis=1)
            z = z + pltpu.roll(z, cg, axis=1)
            y = z + sa * br_ref[t, :, lanes] + v_ref[t, :, lanes] * kr_ref[t, :, lanes]
            acc_ref[g, 0] = sa
            y_ref[t, :, pl.ds(g * cg, cg)] = y[:, :cg]

        for g in range(ngroups):
            def update_k(kb, c, g=g):
                sa = acc_ref[g, 0]
                v2 = v_ref[t, :, pl.ds(g * gw, gw)]
                w8, b8, k8 = rows8(0, g, kb), rows8(2, g, kb), rows8(3, g, kb)
                for j in range(SUBLANES):
                    k = kb * SUBLANES + j
                    st_ref[g, k] = st_ref[g, k] * row(w8, j) + sa * row(b8, j) + v2 * row(k8, j)
                return c

            lax.fori_loop(0, nk // SUBLANES, update_k, 0)
        return carry

    lax.fori_loop(0, SCAN_STEPS, step, 0)


def rwkv_scan(feats, v, br, kr):
    steps, nk, wide = feats[0].shape
    nv = v.shape[1]
    gw = wide // RWKV_GROUPS
    nblk = steps // SCAN_STEPS
    spec = lambda rows, width: pl.BlockSpec((SCAN_STEPS, rows, width), lambda i: (i, 0, 0))
    return pl.pallas_call(
        _rwkv_scan_kernel,
        grid=(nblk,),
        in_specs=[spec(nk, wide)] * 5 + [spec(nv, wide), spec(1, wide), spec(1, wide)],
        out_specs=spec(nv, wide // 2),
        out_shape=jax.ShapeDtypeStruct((steps, nv, wide // 2), F32),
        scratch_shapes=[pltpu.VMEM((RWKV_GROUPS, nk, nv, gw), F32), pltpu.VMEM((RWKV_GROUPS, 2, nv, gw), F32)],
        compiler_params=_params(("arbitrary",), 32),
        name="rwkv_scan",
    )(*feats, v, br, kr)


def _scan_order(dims, a, forward, col_major=False):
    width = a.shape[1]
    lat = a[:dims.m_lat].reshape(dims.B, dims.T, width)
    if col_major:
        lat = lat.reshape(dims.B, dims.T // GRID_W, GRID_W, width).transpose(0, 2, 1, 3).reshape(dims.B, dims.T, width)
    ctx = a[dims.m_lat:].reshape(dims.B, dims.C, width)
    if not forward:
        lat, ctx = lat[:, ::-1], ctx[:, ::-1]
    return jnp.concatenate([ctx, lat], axis=1)


def _to_chains(dims, a_f, a_b, per_head, fold):
    steps = a_f.shape[1]
    a = jnp.stack([a_f, a_b])[..., :RWKV_HEADS * per_head]
    if fold:
        a = a.reshape(2, dims.B, steps, RWKV_GROUPS, RWKV_GROUP_HEADS, 2, per_head // 2)
        a = a.transpose(2, 6, 3, 5, 0, 1, 4)
        return a.reshape(steps, per_head // 2, -1)
    a = a.reshape(2, dims.B, steps, RWKV_GROUPS, 1, RWKV_GROUP_HEADS, per_head)
    a = jnp.broadcast_to(a, (2, dims.B, steps, RWKV_GROUPS, 2, RWKV_GROUP_HEADS, per_head))
    return a.transpose(2, 6, 3, 4, 0, 1, 5).reshape(steps, per_head, -1)


def _from_chains(dims, y):
    steps = y.shape[0]
    y = y.reshape(steps, RWKV_HEAD, RWKV_GROUPS, 2, dims.B, RWKV_GROUP_HEADS).transpose(3, 4, 0, 2, 5, 1)
    return y.reshape(2, dims.B, steps, RWKV_WIDTH)


def rwkv_mixer(dims, proj, mu, w0, w2, a0, a2, k_k, k_a, r_k):
    pf = rwkv_prep(dims, proj, 0, mu, w0, w2, a0, a2, k_k, k_a, r_k)
    pb = rwkv_prep(dims, proj, 1, mu, w0, w2, a0, a2, k_k, k_a, r_k)
    feats = [_to_chains(dims, pf[n], pb[n], RWKV_HEAD, True) for n in range(5)]
    v = _to_chains(dims, pf[5], pb[5], RWKV_HEAD, False)
    br = _to_chains(dims, pf[7], pb[7], 1, False)
    kr = _to_chains(dims, pf[8], pb[8], 1, False)
    y = rwkv_scan(feats, v, br, kr)
    return _from_chains(dims, y), pf[6], pb[6]


S5_WIDTH = 512
S5_GROUPS = S5_WIDTH // S5_GROUP
S5_LANES = S5_GROUPS * S5_STATE
S5_BLOCKS = S5_WIDTH // LANES
S5_STEPS = 32


def _s5_kernel(u_ref, lr_ref, li_ref, wre_ref, wim_ref, cre_ref, cim_ref, y_ref, xr_ref, xi_ref, sr_ref, si_ref,
               *, n_fwd_rows):
    @pl.when(pl.program_id(0) == 0)
    def _():
        sr_ref[...] = jnp.zeros_like(sr_ref)
        si_ref[...] = jnp.zeros_like(si_ref)

    rows = S5_STEPS * SUBLANES
    seg = S5_LANES // S5_BLOCKS
    u = u_ref[...].reshape(rows, S5_WIDTH).astype(BF16)
    backward = lax.broadcasted_iota(jnp.int32, (rows, 1), 0) % SUBLANES >= n_fwd_rows
    for q in range(S5_BLOCKS):
        uq = u[:, q * LANES:(q + 1) * LANES]
        cols = pl.ds(q * seg, seg)
        dot = functools.partial(jnp.dot, preferred_element_type=F32)
        xr_ref[:, cols] = jnp.where(backward, dot(uq, wre_ref[1, q]), dot(uq, wre_ref[0, q]))
        xi_ref[:, cols] = jnp.where(backward, dot(uq, wim_ref[1, q]), dot(uq, wim_ref[0, q]))

    lam_r = lr_ref[...]
    lam_i = li_ref[...]

    def step(t, carry):
        sr, si = carry
        r = pl.ds(pl.multiple_of(t * SUBLANES, SUBLANES), SUBLANES)
        nr = lam_r * sr - lam_i * si + xr_ref[r, :]
        ni = lam_r * si + lam_i * sr + xi_ref[r, :]
        xr_ref[r, :] = nr
        xi_ref[r, :] = ni
        return nr, ni

    sr, si = lax.fori_loop(0, S5_STEPS, step, (sr_ref[...], si_ref[...]))
    sr_ref[...] = sr
    si_ref[...] = si
    for q in range(S5_BLOCKS):
        cols = pl.ds(q * seg, seg)
        yq = _bdot(xr_ref[:, cols], cre_ref[q]) - _bdot(xi_ref[:, cols], cim_ref[q])
        y_ref[:, :, pl.ds(q * LANES, LANES)] = yq.reshape(S5_STEPS, SUBLANES, LANES)


def _s5_maps(lam_re, lam_im, log_dt, b_re, b_im, c_re, c_im):
    lre = jnp.minimum(lam_re, S5_MAX_RE)
    dt = jnp.exp(log_dt)[..., None]
    mag = jnp.exp(lre * dt)
    lbr, lbi = mag * jnp.cos(lam_im * dt), mag * jnp.sin(lam_im * dt)
    den = lre * lre + lam_im * lam_im
    cr = ((lbr - 1.0) * lre + lbi * lam_im) / den
    ci = (lbi * lre - (lbr - 1.0) * lam_im) / den
    bbr = cr[..., None] * b_re[None] - ci[..., None] * b_im[None]
    bbi = cr[..., None] * b_im[None] + ci[..., None] * b_re[None]
    gpb = S5_GROUPS // S5_BLOCKS
    eye = jnp.eye(gpb, dtype=F32)

    def in_map(bb):
        bb = bb.reshape(2, S5_BLOCKS, gpb, S5_STATE, S5_GROUP)
        return jnp.einsum("dqgpc,gh->dqgchp", bb, eye).reshape(2, S5_BLOCKS, LANES, gpb * S5_STATE).astype(BF16)

    def out_map(cc):
        cc = cc.reshape(S5_BLOCKS, gpb, S5_GROUP, S5_STATE)
        return jnp.einsum("qgcp,gh->qgphc", cc, eye).reshape(S5_BLOCKS, gpb * S5_STATE, LANES).astype(BF16)

    return lbr, lbi, in_map(bbr), in_map(bbi), out_map(c_re), out_map(c_im)


def s5_mixer(dims, u_all, lam_re, lam_im, log_dt, b_re, b_im, c_re, c_im):
    assert 2 * dims.B <= SUBLANES
    lbr, lbi, wre, wim, cre, cim = _s5_maps(lam_re, lam_im, log_dt, b_re, b_im, c_re, c_im)
    rows_dir = jnp.minimum(jnp.arange(SUBLANES) // dims.B, 1)
    lam_r = lbr.reshape(2, S5_LANES)[rows_dir]
    lam_i = lbi.reshape(2, S5_LANES)[rows_dir]
    u = jnp.stack([_scan_order(dims, u_all, True, col_major=True), _scan_order(dims, u_all, False, col_major=True)])
    steps = u.shape[2]
    u = u.reshape(2 * dims.B, steps, S5_WIDTH).transpose(1, 0, 2)
    u = jnp.pad(u, ((0, 0), (0, SUBLANES - 2 * dims.B), (0, 0)))
    full = lambda shape: pl.BlockSpec(shape, lambda i: (0,) * len(shape))
    rows = S5_STEPS * SUBLANES
    y = pl.pallas_call(
        functools.partial(_s5_kernel, n_fwd_rows=dims.B),
        grid=(steps // S5_STEPS,),
        in_specs=[pl.BlockSpec((S5_STEPS, SUBLANES, S5_WIDTH), lambda i: (i, 0, 0)),
                  full((SUBLANES, S5_LANES)), full((SUBLANES, S5_LANES)),
                  full(wre.shape), full(wim.shape), full(cre.shape), full(cim.shape)],
        out_specs=pl.BlockSpec((S5_STEPS, SUBLANES, S5_WIDTH), lambda i: (i, 0, 0)),
        out_shape=jax.ShapeDtypeStruct((steps, SUBLANES, S5_WIDTH), F32),
        scratch_shapes=[pltpu.VMEM((rows, S5_LANES), F32), pltpu.VMEM((rows, S5_LANES), F32),
                        pltpu.VMEM((SUBLANES, S5_LANES), F32), pltpu.VMEM((SUBLANES, S5_LANES), F32)],
        compiler_params=_params(("arbitrary",), 32),
        name="s5_scan",
    )(u, lam_r, lam_i, wre, wim, cre, cim)

    def latent(yd, forward):
        yd = yd[dims.C:].transpose(1, 0, 2)
        if not forward:
            yd = yd[:, ::-1]
        yd = yd.reshape(dims.B, GRID_W, dims.T // GRID_W, S5_WIDTH).transpose(0, 2, 1, 3)
        return yd.reshape(dims.m_lat, S5_WIDTH)

    return latent(y[:, :dims.B], True), latent(y[:, dims.B:2 * dims.B], False)


def _gelu_tanh(x):
    return 0.5 * x * (1.0 + jnp.tanh(math.sqrt(2.0 / math.pi) * (x + 0.044715 * x * x * x)))


def _odd_out_kernel(x_ref, yf_ref, yb_ref, bf_ref, bb_ref, gl_ref, sf_ref, sb_ref, u_ref, lng, lnb, g2_ref, dsk,
                    glw, glb, e_ref, et_ref, gate_ref, w_ref, o_ref, a_ref):
    @pl.when(pl.program_id(1) == 0)
    def _():
        e = e_ref[...]
        et = et_ref[...]
        tm = yb_ref.shape[0]
        rows = lax.broadcasted_iota(jnp.int32, (tm, tm), 0)
        cols = lax.broadcasted_iota(jnp.int32, (tm, tm), 1)
        flip = (rows + cols == tm - 1).astype(BF16)
        y = yf_ref[...] + _dot_sel(flip, yb_ref[...])
        mean = _dot_rsel(_dot_rsel(y, e) * (1.0 / RWKV_HEAD), et)
        cen = y - mean
        var = _dot_rsel(cen * cen, e) * (1.0 / RWKV_HEAD)
        yn = cen * _dot_rsel(lax.rsqrt(var + RWKV_GN_EPS), et) * lng[...] + lnb[...]
        yn = yn + bf_ref[...] + bb_ref[...]
        a_ref[:, pl.ds(0, RWKV_WIDTH)] = (yn * _dot3(_sigmoid(gl_ref[...]), g2_ref[...])).astype(BF16)
        ys = _gelu_tanh(sf_ref[...] + sb_ref[...] + dsk[...] * u_ref[...])
        a_ref[:, pl.ds(RWKV_WIDTH, S5_WIDTH)] = (ys * _sigmoid(_dot3(ys, glw[...]) + glb[...])).astype(BF16)

    m = jnp.dot(a_ref[...], w_ref[...].astype(BF16), preferred_element_type=F32)
    o_ref[...] = x_ref[...] + gate_ref[...] * m


def odd_out(dims, x_all, proj, u_all, y_scan, bonus_f, bonus_b, s_f, s_b, ln_g, ln_b, g2, d_skip, glu_w, glu_b,
            w_out, gate, tn=512):
    d = x_all.shape[1]
    tm = TOKEN_BLOCK
    wd = RWKV_WIDTH
    nlb = dims.nlb
    e, et = _head_sum_matrices()
    rowblk = lambda width: pl.BlockSpec((tm, width), lambda i, j: (i, 0))
    y_fwd = pl.BlockSpec((None, None, tm, wd), lambda i, j: (0, i // nlb, 1 + i % nlb, 0))
    y_bwd = pl.BlockSpec((None, None, tm, wd), lambda i, j: (1, i // nlb, nlb - i % nlb, 0))
    vec = lambda width: pl.BlockSpec((1, width), lambda i, j: (0, 0))
    mat = lambda r, c: pl.BlockSpec((r, c), lambda i, j: (0, 0))
    return pl.pallas_call(
        _odd_out_kernel,
        grid=(dims.m_lat // tm, d // tn),
        in_specs=[pl.BlockSpec((tm, tn), lambda i, j: (i, j)),
                  y_fwd, y_bwd, rowblk(wd), rowblk(wd),
                  pl.BlockSpec((tm, LANES), lambda i, j: (i, ODD_GL // LANES)),
                  rowblk(S5_WIDTH), rowblk(S5_WIDTH), rowblk(S5_WIDTH),
                  vec(wd), vec(wd), mat(LANES, wd), vec(S5_WIDTH), mat(S5_WIDTH, S5_WIDTH), vec(S5_WIDTH),
                  mat(wd, LANES), mat(LANES, wd),
                  pl.BlockSpec((None, 1, tn), lambda i, j: (dims.sample_of_tile(i, tm), 0, j)),
                  pl.BlockSpec((d, tn), lambda i, j: (0, j))],
        out_specs=pl.BlockSpec((tm, tn), lambda i, j: (i, j)),
        out_shape=jax.ShapeDtypeStruct((dims.m_lat, d), F32),
        scratch_shapes=[pltpu.VMEM((tm, d), BF16)],
        compiler_params=_params(("parallel", "arbitrary"), 56),
        name="odd_out",
    )(x_all, y_scan, y_scan, bonus_f, bonus_b, proj, s_f, s_b, u_all, ln_g.reshape(1, wd), ln_b.reshape(1, wd), g2,
      d_skip.reshape(1, S5_WIDTH), glu_w, glu_b.reshape(1, S5_WIDTH), e, et, gate, w_out)


def odd_mixer_layer(dims, x_all, mod, norm_g, w_in, mu, w0, w2, a0, a2, g2, k_k, k_a, r_k, ln_g, ln_b,
                    lam_re, lam_im, log_dt, b_re, b_im, c_re, c_im, d_skip, glu_w, glu_b, w_out):
    w_pad = jnp.pad(w_in, ((0, 0), (0, ODD_COLS - w_in.shape[1])))
    proj = normmod_matmul(dims, x_all, norm_g, mod[0], mod[1], w_pad, tn=512)
    y_scan, bonus_f, bonus_b = rwkv_mixer(dims, proj, mu, w0, w2, a0, a2, k_k, k_a, r_k)
    u_all = proj[:, ODD_S5:ODD_S5 + S5_WIDTH]
    s_f, s_b = s5_mixer(dims, u_all, lam_re, lam_im, log_dt, b_re, b_im, c_re, c_im)
    return odd_out(dims, x_all, proj, u_all, y_scan, bonus_f, bonus_b, s_f, s_b, ln_g, ln_b, g2, d_skip, glu_w,
                   glu_b, w_out, mod[2])


def kernel(x, c, ctx, c_ctx, ada_w, ada_b, norm1_g, norm2_g, even_w_in, gla_gate_w, gla_gate_b, gla_norm_g, conv_w, conv_b, conv_ln_g, conv_ln_b, even_w_out, odd_w_in, rwkv_mu, rwkv_w0, rwkv_w2, rwkv_a0, rwkv_a2, rwkv_g2, rwkv_k_k, rwkv_k_a, rwkv_r_k, rwkv_ln_g, rwkv_ln_b, s5_lam_re, s5_lam_im, s5_log_dt, s5_b_re, s5_b_im, s5_c_re, s5_c_im, s5_d, s5_glu_w, s5_glu_b, odd_w_out, router_w, router_b, moe_w1, moe_b1, moe_w2, moe_b2, final_g):
    B, T, D = x.shape
    C = ctx.shape[1]
    assert ada_w.shape[0] == 2, "one even layer followed by one (final) odd layer"
    dims = Dims(B, T, C)
    cvec = jnp.zeros((SUBLANES, D), F32).at[:B].set(c).at[B].set(c_ctx)
    tab = adaln_table(cvec, ada_w, ada_b)
    x_all = jnp.concatenate([x.reshape(-1, D), ctx.reshape(-1, D)], axis=0)
    mod = tab[0].reshape(SUBLANES, N_MOD, 1, D).transpose(1, 0, 2, 3)
    x_all = even_mixer_layer(dims, x_all, mod, norm1_g[0], even_w_in[0], gla_gate_w[0], gla_gate_b[0],
                             gla_norm_g[0], conv_w[0], conv_b[0], conv_ln_g[0], conv_ln_b[0], even_w_out[0])
    x_all = moe_layer(dims, x_all, dims.m_all, mod, norm2_g[0], router_w[0], router_b[0], 0, moe_w1, moe_b1[0],
                      moe_w2, moe_b2[0])
    mod = tab[1].reshape(SUBLANES, N_MOD, 1, D).transpose(1, 0, 2, 3)
    x_lat = odd_mixer_layer(dims, x_all, mod, norm1_g[1], odd_w_in[0], rwkv_mu[0], rwkv_w0[0], rwkv_w2[0],
                            rwkv_a0[0], rwkv_a2[0], rwkv_g2[0], rwkv_k_k[0], rwkv_k_a[0], rwkv_r_k[0],
                            rwkv_ln_g[0], rwkv_ln_b[0], s5_lam_re[0], s5_lam_im[0], s5_log_dt[0], s5_b_re[0],
                            s5_b_im[0], s5_c_re[0], s5_c_im[0], s5_d[0], s5_glu_w[0], s5_glu_b[0], odd_w_out[0])
    x_lat = moe_layer(dims, x_lat, dims.m_lat, mod, norm2_g[1], router_w[1], router_b[1], 1, moe_w1, moe_b1[1],
                      moe_w2, moe_b2[1], final_g=final_g)
    return x_lat.reshape(B, T, D)
```

```python
import functools
import math

import jax
import jax.numpy as jnp
import numpy as np
from jax import lax
from jax.experimental import pallas as pl
from jax.experimental.pallas import tpu as pltpu

F32 = jnp.float32
BF16 = jnp.bfloat16

NORM_EPS = 1e-6
GRID_W = 64
N_MOD = 6
LANES = 128
SUBLANES = 8
VMEM_BYTES = 64 << 20

GLA_HEADS = 4
GLA_DK = 128
GLA_DV = 256
GLA_RANK = 16
GLA_NORMALIZER = 16.0
GLA_CHUNK = 64
CONV_WIDTH = 31
CONV_PAD = 16
TOKEN_BLOCK = 256

RWKV_HEAD = 64
RWKV_GN_EPS = 64e-5
S5_GROUP = 16
S5_STATE = 64
S5_MAX_RE = -1e-4

N_EXPERTS = 32
TOP_K = 4
SWIGLU_ALPHA = 1.702
SWIGLU_LIMIT = 7.0


def _params(semantics, vmem_mb):
    return pltpu.CompilerParams(dimension_semantics=semantics, vmem_limit_bytes=vmem_mb << 20)


def _bdot(a, b):
    return jnp.dot(a.astype(BF16), b.astype(BF16), preferred_element_type=F32)


def _split2(x):
    hi = x.astype(BF16)
    lo = (x - hi.astype(F32)).astype(BF16)
    return hi, lo


def _dot3(a, b):
    ah, al = _split2(a)
    bh, bl = _split2(b)
    d = functools.partial(jnp.dot, preferred_element_type=F32)
    return d(ah, bh) + d(al, bh) + d(ah, bl)


def _dot_sel(sel, b):
    b1 = b.astype(BF16)
    r1 = b - b1.astype(F32)
    b2 = r1.astype(BF16)
    b3 = (r1 - b2.astype(F32)).astype(BF16)
    d = functools.partial(jnp.dot, preferred_element_type=F32)
    return d(sel, b1) + d(sel, b2) + d(sel, b3)


def _dot_rsel(a, sel):
    a1 = a.astype(BF16)
    r1 = a - a1.astype(F32)
    a2 = r1.astype(BF16)
    a3 = (r1 - a2.astype(F32)).astype(BF16)
    d = functools.partial(jnp.dot, preferred_element_type=F32)
    return d(a1, sel) + d(a2, sel) + d(a3, sel)


def _sigmoid(x):
    return 1.0 / (1.0 + jnp.exp(-x))


def _silu(x):
    return x * _sigmoid(x)


def _log_sigmoid(x):
    return jnp.minimum(x, 0.0) - jnp.log(1.0 + jnp.exp(-jnp.abs(x)))


def _softplus(x):
    return jnp.maximum(x, 0.0) + jnp.log(1.0 + jnp.exp(-jnp.abs(x)))


class Dims:
    def __init__(self, batch, seq, ctx_len):
        self.B, self.T, self.C = batch, seq, ctx_len
        self.m_lat = batch * seq
        self.m_ctx = batch * ctx_len
        self.m_all = self.m_lat + self.m_ctx
        assert ctx_len == TOKEN_BLOCK and seq % TOKEN_BLOCK == 0 and seq % GRID_W == 0
        self.nlb = seq // TOKEN_BLOCK
        self.tm = math.gcd(math.gcd(seq, self.m_ctx), 1024)

    def sample_of_tile(self, i, tm):
        return jnp.where(i * tm < self.m_lat, (i * tm) // self.T, self.B)


def _adaln_kernel(c_ref, w_ref, b_ref, o_ref):
    o_ref[...] = _bdot(_silu(c_ref[...]), w_ref[...]) + b_ref[...]


def adaln_table(cvec, ada_w, ada_b):
    depth, d, n = ada_w.shape
    tn = 1024
    return pl.pallas_call(
        _adaln_kernel,
        grid=(depth, n // tn),
        in_specs=[
            pl.BlockSpec((SUBLANES, d), lambda l, j: (0, 0)),
            pl.BlockSpec((None, d, tn), lambda l, j: (l, 0, j)),
            pl.BlockSpec((None, 1, tn), lambda l, j: (l, 0, j)),
        ],
        out_specs=pl.BlockSpec((None, SUBLANES, tn), lambda l, j: (l, 0, j)),
        out_shape=jax.ShapeDtypeStruct((depth, SUBLANES, n), F32),
        compiler_params=_params(("parallel", "parallel"), 40),
        name="adaln_table",
    )(cvec, ada_w, ada_b.reshape(depth, 1, n))


def _normmod(x, g, shift, scale):
    y = x * lax.rsqrt(jnp.mean(x * x, axis=-1, keepdims=True) + NORM_EPS) * g
    return y * (1.0 + scale) + shift


def _normmod_mm_kernel(x_ref, g_ref, sh_ref, sc_ref, w_ref, o_ref, h_ref):
    @pl.when(pl.program_id(1) == 0)
    def _():
        h_ref[...] = _normmod(x_ref[...], g_ref[...], sh_ref[...], sc_ref[...]).astype(BF16)

    o_ref[...] = jnp.dot(h_ref[...], w_ref[...].astype(BF16), preferred_element_type=F32)


def normmod_matmul(dims, x_all, g, shift, scale, w, tn):
    m, d = x_all.shape
    n = w.shape[1]
    tm = dims.tm
    samp = lambda i, j: (dims.sample_of_tile(i, tm), 0, 0)
    return pl.pallas_call(
        _normmod_mm_kernel,
        grid=(m // tm, n // tn),
        in_specs=[
            pl.BlockSpec((tm, d), lambda i, j: (i, 0)),
            pl.BlockSpec((1, d), lambda i, j: (0, 0)),
            pl.BlockSpec((None, 1, d), samp),
            pl.BlockSpec((None, 1, d), samp),
            pl.BlockSpec((d, tn), lambda i, j: (0, j)),
        ],
        out_specs=pl.BlockSpec((tm, tn), lambda i, j: (i, j)),
        out_shape=jax.ShapeDtypeStruct((m, n), F32),
        scratch_shapes=[pltpu.VMEM((tm, d), BF16)],
        compiler_params=_params(("parallel", "arbitrary"), 56),
        name="normmod_matmul",
    )(x_all, g.reshape(1, d), shift, scale, w)


EVEN_Q, EVEN_K, EVEN_V, EVEN_OG, EVEN_CA, EVEN_CB, EVEN_LRF, EVEN_LRB = 0, 512, 1024, 2048, 3072, 4096, 5120, 5248
EVEN_COLS = 5376


def _gla_chunk(q, k, v, lr, gw, gb, st_ref, forward):
    length = q.shape[0]
    row = lax.broadcasted_iota(jnp.int32, (length, length), 0)
    col = lax.broadcasted_iota(jnp.int32, (length, length), 1)
    tri = (col <= row) if forward else (col >= row)
    g = _log_sigmoid(_dot3(lr, gw) + gb) * (1.0 / GLA_NORMALIZER)
    cum = _dot_sel(tri.astype(BF16), g)
    tot = cum[length - 1:length, :] if forward else cum[0:1, :]
    q_dec = (q * (GLA_DK ** -0.5) * jnp.exp(cum)).astype(BF16)
    k_inv = (k * jnp.exp(-cum)).astype(BF16)
    k_st = (k * jnp.exp(tot - cum)).astype(BF16)
    scores = lax.dot_general(q_dec, k_inv, (((1,), (1,)), ((), ())), preferred_element_type=F32)
    scores = jnp.where(tri, scores, 0.0)
    st = st_ref[...]
    o = _bdot(scores, v) + lax.dot_general(q_dec, st.astype(BF16), (((1,), (1,)), ((), ())),
                                           preferred_element_type=F32)
    st_ref[...] = st * jnp.exp(tot) + lax.dot_general(
        v.astype(BF16), k_st, (((0,), (0,)), ((), ())), preferred_element_type=F32)
    return o


def _gla_kernel(qf, kf, vf, lf, qb, kb, vb, lb, gwf, gbf, gwb, gbb, of, ob, sf, sb):
    @pl.when(pl.program_id(2) == 0)
    def _():
        sf[...] = jnp.zeros_like(sf)
        sb[...] = jnp.zeros_like(sb)

    nchunk = TOKEN_BLOCK // GLA_CHUNK
    for c in range(nchunk):
        r = pl.ds(c * GLA_CHUNK, GLA_CHUNK)
        of[r, :] = _gla_chunk(qf[r, :], kf[r, :], vf[r, :], lf[r, :], gwf[...], gbf[...], sf, True)
        r = pl.ds((nchunk - 1 - c) * GLA_CHUNK, GLA_CHUNK)
        ob[r, :] = _gla_chunk(qb[r, :], kb[r, :], vb[r, :], lb[r, :], gwb[...], gbb[...], sb, False)


def gla_scan(dims, proj, gate_w_pad, gate_b):
    B, nlb = dims.B, dims.nlb
    blk = TOKEN_BLOCK

    def fblk(b, s):
        return jnp.where(s == 0, B * nlb + b, b * nlb + s - 1)

    def bblk(b, s):
        return jnp.where(s == 0, B * nlb + b, b * nlb + nlb - s)

    def col(blk_fn, width, start):
        return pl.BlockSpec((blk, width), lambda b, h, s: (blk_fn(b, s), start // width + h))

    def lr(blk_fn, start):
        return pl.BlockSpec((blk, LANES), lambda b, h, s: (blk_fn(b, s), start // LANES))

    def gw(d):
        return pl.BlockSpec((None, LANES, GLA_DK), lambda b, h, s: (d, 0, h))

    def gb(d):
        return pl.BlockSpec((None, 1, GLA_DK), lambda b, h, s: (d, 0, h))

    out = jax.ShapeDtypeStruct((dims.m_all, GLA_HEADS * GLA_DV), F32)
    return pl.pallas_call(
        _gla_kernel,
        grid=(B, GLA_HEADS, nlb + 1),
        in_specs=[
            col(fblk, GLA_DK, EVEN_Q), col(fblk, GLA_DK, EVEN_K), col(fblk, GLA_DV, EVEN_V), lr(fblk, EVEN_LRF),
            col(bblk, GLA_DK, EVEN_Q), col(bblk, GLA_DK, EVEN_K), col(bblk, GLA_DV, EVEN_V), lr(bblk, EVEN_LRB),
            gw(0), gb(0), gw(1), gb(1),
        ],
        out_specs=[
            pl.BlockSpec((blk, GLA_DV), lambda b, h, s: (fblk(b, s), h)),
            pl.BlockSpec((blk, GLA_DV), lambda b, h, s: (bblk(b, s), h)),
        ],
        out_shape=[out, out],
        scratch_shapes=[pltpu.VMEM((GLA_DV, GLA_DK), F32), pltpu.VMEM((GLA_DV, GLA_DK), F32)],
        compiler_params=_params(("parallel", "parallel", "arbitrary"), 32),
        name="gla_scan",
    )(proj, proj, proj, proj, proj, proj, proj, proj, gate_w_pad, gate_b, gate_w_pad, gate_b)


def _conv_kernel(ca_ref, cb_ref, w_ref, b_ref, g_ref, beta_ref, o_ref, pad_ref, *, row_len):
    blk, ch = o_ref.shape
    seg = row_len + 2 * CONV_PAD
    nrows = blk // row_len
    h = ca_ref[...] * _sigmoid(cb_ref[...])
    for r in range(nrows):
        pad_ref[pl.ds(r * seg, CONV_PAD), :] = jnp.zeros((CONV_PAD, ch), F32)
        pad_ref[pl.ds(r * seg + CONV_PAD, row_len), :] = h[r * row_len:(r + 1) * row_len, :]
        pad_ref[pl.ds(r * seg + CONV_PAD + row_len, CONV_PAD), :] = jnp.zeros((CONV_PAD, ch), F32)
    half = CONV_WIDTH // 2
    sub = 64
    cw = 256
    for r in range(nrows):
        for m in range(row_len // sub):
            for c in range(ch // cw):
                cs = pl.ds(c * cw, cw)
                acc = jnp.zeros((sub, cw), F32)
                base = r * seg + CONV_PAD + m * sub - half
                for j in range(CONV_WIDTH):
                    acc = acc + w_ref[pl.ds(j, 1), cs] * pad_ref[pl.ds(base + j, sub), cs]
                o_ref[pl.ds(r * row_len + m * sub, sub), cs] = acc
    hf = o_ref[...] + b_ref[...]
    mean = jnp.mean(hf, axis=-1, keepdims=True)
    cen = hf - mean
    var = jnp.mean(cen * cen, axis=-1, keepdims=True)
    o_ref[...] = _silu(cen * lax.rsqrt(var + NORM_EPS) * g_ref[...] + beta_ref[...])


def conformer_conv(dims, proj, conv_w, conv_b, ln_g, ln_b):
    ch = conv_w.shape[1]
    blk = TOKEN_BLOCK
    n_lat = dims.m_lat // blk

    def run(row_len, nblk, blk0):
        vec = lambda: pl.BlockSpec((1, ch), lambda i: (0, 0))
        return pl.pallas_call(
            functools.partial(_conv_kernel, row_len=row_len),
            grid=(nblk,),
            in_specs=[
                pl.BlockSpec((blk, ch), lambda i: (blk0 + i, EVEN_CA // ch)),
                pl.BlockSpec((blk, ch), lambda i: (blk0 + i, EVEN_CB // ch)),
                pl.BlockSpec((CONV_WIDTH, ch), lambda i: (0, 0)),
                vec(), vec(), vec(),
            ],
            out_specs=pl.BlockSpec((blk, ch), lambda i: (i, 0)),
            out_shape=jax.ShapeDtypeStruct((nblk * blk, ch), F32),
            scratch_shapes=[pltpu.VMEM((blk // row_len * (row_len + 2 * CONV_PAD), ch), F32)],
            compiler_params=_params(("parallel",), 32),
            name="conformer_conv",
        )(proj, proj, conv_w, conv_b.reshape(1, ch), ln_g.reshape(1, ch), ln_b.reshape(1, ch))

    lat = run(GRID_W, n_lat, 0)
    ctx = run(dims.C, dims.m_ctx // blk, n_lat)
    return jnp.concatenate([lat, ctx], axis=0)


def _even_out_kernel(x_ref, of_ref, ob_ref, og_ref, cv_ref, ng_ref, gate_ref, w_ref, o_ref, a_ref):
    @pl.when(pl.program_id(1) == 0)
    def _():
        for h in range(GLA_HEADS):
            cs = pl.ds(h * GLA_DV, GLA_DV)
            o = of_ref[:, cs] + ob_ref[:, cs]
            y = o * lax.rsqrt(jnp.mean(o * o, axis=-1, keepdims=True) + NORM_EPS) * ng_ref[...]
            a_ref[:, cs] = (y * _silu(og_ref[:, cs])).astype(BF16)
        nv = GLA_HEADS * GLA_DV
        a_ref[:, pl.ds(nv, cv_ref.shape[1])] = cv_ref[...].astype(BF16)

    m = jnp.dot(a_ref[...], w_ref[...].astype(BF16), preferred_element_type=F32)
    o_ref[...] = x_ref[...] + gate_ref[...] * m


def even_out(dims, x_all, proj, o_f, o_b, conv_out, gla_norm_g, w_out, gate, tn=512):
    m, d = x_all.shape
    tm = dims.tm
    nv = GLA_HEADS * GLA_DV
    return pl.pallas_call(
        _even_out_kernel,
        grid=(m // tm, d // tn),
        in_specs=[
            pl.BlockSpec((tm, tn), lambda i, j: (i, j)),
            pl.BlockSpec((tm, nv), lambda i, j: (i, 0)),
            pl.BlockSpec((tm, nv), lambda i, j: (i, 0)),
            pl.BlockSpec((tm, nv), lambda i, j: (i, EVEN_OG // nv)),
            pl.BlockSpec((tm, d - nv), lambda i, j: (i, 0)),
            pl.BlockSpec((1, GLA_DV), lambda i, j: (0, 0)),
            pl.BlockSpec((None, 1, tn), lambda i, j: (dims.sample_of_tile(i, tm), 0, j)),
            pl.BlockSpec((d, tn), lambda i, j: (0, j)),
        ],
        out_specs=pl.BlockSpec((tm, tn), lambda i, j: (i, j)),
        out_shape=jax.ShapeDtypeStruct((m, d), F32),
        scratch_shapes=[pltpu.VMEM((tm, d), BF16)],
        compiler_params=_params(("parallel", "arbitrary"), 56),
        name="even_out",
    )(x_all, o_f, o_b, proj, conv_out, gla_norm_g.reshape(1, GLA_DV), gate, w_out)


def even_weights(w_in, gate_w):
    d = w_in.shape[0]
    kw, vw = GLA_HEADS * GLA_DK, GLA_HEADS * GLA_DV
    main = 2 * kw + 2 * vw
    lr_f = w_in[:, main:main + GLA_RANK]
    lr_b = w_in[:, main + GLA_RANK:main + 2 * GLA_RANK]
    zpad = jnp.zeros((d, LANES - GLA_RANK), F32)
    w_pad = jnp.concatenate([w_in[:, :main], w_in[:, main + 2 * GLA_RANK:], lr_f, zpad, lr_b, zpad], axis=1)
    assert w_pad.shape[1] == EVEN_COLS
    gw_pad = jnp.pad(gate_w, ((0, 0), (0, LANES - GLA_RANK), (0, 0)))
    return w_pad, gw_pad


def even_mixer_layer(dims, x_all, mod, norm_g, w_in, gate_w, gate_b, gla_norm_g, conv_w, conv_b, conv_ln_g,
                     conv_ln_b, w_out):
    w_pad, gw_pad = even_weights(w_in, gate_w)
    proj = normmod_matmul(dims, x_all, norm_g, mod[0], mod[1], w_pad, tn=768)
    o_f, o_b = gla_scan(dims, proj, gw_pad, gate_b.reshape(2, 1, -1))
    conv_out = conformer_conv(dims, proj, conv_w, conv_b, conv_ln_g, conv_ln_b)
    return even_out(dims, x_all, proj, o_f, o_b, conv_out, gla_norm_g, w_out, mod[2])


EXPERT_TILE = 512
UP_CHUNK = 1024
DOWN_CHUNK = 1024
ROW_TILE = 256
NEG_INF = float("-inf")


def _router_kernel(x_ref, g_ref, sh_ref, sc_ref, rw_ref, rb_ref, f_ref, ti_ref, tw_ref, pos_ref, cnt_ref, carry_ref):
    @pl.when(pl.program_id(0) == 0)
    def _():
        carry_ref[...] = jnp.zeros_like(carry_ref)

    f = _normmod(x_ref[...], g_ref[...], sh_ref[...], sc_ref[...])
    f_ref[...] = f
    tm = f.shape[0]
    work = _dot3(f, rw_ref[...]) + rb_ref[...]
    lane = lax.broadcasted_iota(jnp.int32, (tm, LANES), 1)
    vals, idxs = [], []
    for _ in range(TOP_K):
        m = jnp.max(work, axis=-1, keepdims=True)
        idx = jnp.min(jnp.where(work == m, lane, LANES), axis=-1, keepdims=True)
        vals.append(m)
        idxs.append(idx)
        work = jnp.where(lane == idx, NEG_INF, work)
    es = [jnp.exp(v - vals[0]) for v in vals]
    denom = es[0] + es[1] + es[2] + es[3]
    sel = (work == NEG_INF).astype(F32)
    row = lax.broadcasted_iota(jnp.int32, (tm, tm), 0)
    col = lax.broadcasted_iota(jnp.int32, (tm, tm), 1)
    before = jnp.dot((col < row).astype(BF16), sel.astype(BF16), preferred_element_type=F32)
    rank = before + carry_ref[...]
    ti = jnp.zeros((tm, LANES), jnp.int32)
    tw = jnp.zeros((tm, LANES), F32)
    pos = jnp.zeros((tm, LANES), jnp.int32)
    for k in range(TOP_K):
        pk = jnp.sum(jnp.where(lane == idxs[k], rank, 0.0), axis=-1, keepdims=True).astype(jnp.int32)
        ti = jnp.where(lane == k, idxs[k], ti)
        tw = jnp.where(lane == k, es[k] / denom, tw)
        pos = jnp.where(lane == k, pk, pos)
    ti_ref[...] = ti
    tw_ref[...] = tw
    pos_ref[...] = pos
    carry_ref[...] = carry_ref[...] + jnp.sum(sel, axis=0, keepdims=True)
    cnt_ref[...] = carry_ref[...]


def moe_router(dims, x_all, m_rows, g, shift, scale, router_w, router_b):
    d = x_all.shape[1]
    tm = min(dims.tm, 512)
    rw = jnp.pad(router_w, ((0, 0), (0, LANES - N_EXPERTS)))
    rb = jnp.concatenate([router_b, jnp.full((LANES - N_EXPERTS,), -1e30, F32)]).reshape(1, LANES)
    samp = lambda i: (dims.sample_of_tile(i, tm), 0, 0)
    lane_out = lambda dt: jax.ShapeDtypeStruct((m_rows, LANES), dt)
    lane_spec = lambda: pl.BlockSpec((tm, LANES), lambda i: (i, 0))
    return pl.pallas_call(
        _router_kernel,
        grid=(m_rows // tm,),
        in_specs=[
            pl.BlockSpec((tm, d), lambda i: (i, 0)),
            pl.BlockSpec((1, d), lambda i: (0, 0)),
            pl.BlockSpec((None, 1, d), samp),
            pl.BlockSpec((None, 1, d), samp),
            pl.BlockSpec((d, LANES), lambda i: (0, 0)),
            pl.BlockSpec((1, LANES), lambda i: (0, 0)),
        ],
        out_specs=[pl.BlockSpec((tm, d), lambda i: (i, 0)), lane_spec(), lane_spec(), lane_spec(),
                   pl.BlockSpec((1, LANES), lambda i: (0, 0))],
        out_shape=[jax.ShapeDtypeStruct((m_rows, d), F32), lane_out(jnp.int32), lane_out(F32), lane_out(jnp.int32),
                   jax.ShapeDtypeStruct((1, LANES), F32)],
        scratch_shapes=[pltpu.VMEM((1, LANES), F32)],
        compiler_params=_params(("arbitrary",), 40),
        name="moe_router",
    )(x_all, g.reshape(1, d), shift, scale, rw, rb)


def _dispatch_kernel(slot_ref, f_ref, xs_in, xs_hbm, sem):
    del xs_in

    def copy(r, k):
        return pltpu.make_async_copy(f_ref.at[pl.ds(r, 1)], xs_hbm.at[pl.ds(slot_ref[r * TOP_K + k], 1)], sem)

    def issue(r, carry):
        for k in range(TOP_K):
            copy(r, k).start()
        return carry

    def drain(r, carry):
        for k in range(TOP_K):
            copy(r, k).wait()
        return carry

    lax.fori_loop(0, ROW_TILE, issue, 0)
    lax.fori_loop(0, ROW_TILE, drain, 0)


def moe_dispatch(f, slot_flat, n_slots):
    m, d = f.shape
    xs0 = jnp.zeros((n_slots, d), F32)
    return pl.pallas_call(
        _dispatch_kernel,
        grid=(m // ROW_TILE,),
        in_specs=[
            pl.BlockSpec((ROW_TILE * TOP_K,), lambda i: (i,), memory_space=pltpu.SMEM),
            pl.BlockSpec((ROW_TILE, d), lambda i: (i, 0)),
            pl.BlockSpec(memory_space=pl.ANY),
        ],
        out_specs=pl.BlockSpec(memory_space=pl.ANY),
        out_shape=jax.ShapeDtypeStruct((n_slots, d), F32),
        scratch_shapes=[pltpu.SemaphoreType.DMA(())],
        input_output_aliases={2: 0},
        compiler_params=_params(("arbitrary",), 16),
        name="moe_dispatch",
    )(slot_flat, f, xs0)


def _expert_changed(te_ref):
    t = pl.program_id(1)
    return jnp.logical_or(t == 0, te_ref[t] != te_ref[jnp.maximum(t - 1, 0)])


def _expert_up_kernel(te_ref, tv_ref, x_ref, w_ref, b_ref, h_ref, wb_ref):
    half = UP_CHUNK // 2

    @pl.when(_expert_changed(te_ref))
    def _():
        wb_ref[...] = w_ref[...].astype(BF16)

    @pl.when(tv_ref[pl.program_id(1)] != 0)
    def _():
        h = jnp.dot(x_ref[...].astype(BF16), wb_ref[...], preferred_element_type=F32) + b_ref[...]
        even = lax.broadcasted_iota(jnp.int32, (h.shape[0], half), 1) % 2 == 0

        def act(hh):
            glu = jnp.minimum(hh, SWIGLU_LIMIT)
            lin = jnp.clip(hh, -SWIGLU_LIMIT, SWIGLU_LIMIT) + 1.0
            return glu * _sigmoid(SWIGLU_ALPHA * glu) * pltpu.roll(lin, half - 1, axis=1)

        packed = jnp.where(even, act(h[:, :half]), pltpu.roll(act(h[:, half:]), 1, axis=1))
        h_ref[...] = packed.astype(BF16)

    @pl.when(tv_ref[pl.program_id(1)] == 0)
    def _():
        h_ref[...] = jnp.zeros_like(h_ref)


def _expert_down_kernel(te_ref, tv_ref, h_ref, w_ref, b_ref, y_ref, stage_ref, wb_ref):
    half = UP_CHUNK // 4

    @pl.when(_expert_changed(te_ref))
    def _():
        for c in range(stage_ref.shape[0]):
            cols = pl.ds(c * LANES, LANES)
            for blk in range(w_ref.shape[0] // (2 * half)):
                for s in range(2):
                    stage_ref[c, pl.ds(blk * 2 * half + s, half, stride=2), :] = w_ref[
                        pl.ds((2 * blk + s) * half, half), cols]
            wb_ref[:, cols] = stage_ref[c].astype(BF16)

    @pl.when(tv_ref[pl.program_id(1)] != 0)
    def _():
        y_ref[...] = jnp.dot(h_ref[...], wb_ref[...], preferred_element_type=F32) + b_ref[...]

    @pl.when(tv_ref[pl.program_id(1)] == 0)
    def _():
        y_ref[...] = jnp.zeros_like(y_ref)


def moe_experts(xs, tile_expert, tile_valid, layer, w1, b1, w2, b2):
    n_slots, d = xs.shape
    _, n_exp, _, two_f = w1.shape
    f = two_f // 2
    nt = n_slots // EXPERT_TILE
    grid_spec = pltpu.PrefetchScalarGridSpec(
        num_scalar_prefetch=2,
        grid=(two_f // UP_CHUNK, nt),
        in_specs=[
            pl.BlockSpec((EXPERT_TILE, d), lambda j, t, te, tv: (t, 0)),
            pl.BlockSpec((None, None, d, UP_CHUNK), lambda j, t, te, tv: (layer, te[t], 0, j)),
            pl.BlockSpec((None, 1, UP_CHUNK), lambda j, t, te, tv: (te[t], 0, j)),
        ],
        out_specs=pl.BlockSpec((EXPERT_TILE, UP_CHUNK // 2), lambda j, t, te, tv: (t, j)),
        scratch_shapes=[pltpu.VMEM((d, UP_CHUNK), BF16)],
    )
    hidden = pl.pallas_call(
        _expert_up_kernel,
        grid_spec=grid_spec,
        out_shape=jax.ShapeDtypeStruct((n_slots, f), BF16),
        compiler_params=_params(("arbitrary", "arbitrary"), 48),
        name="moe_expert_up",
    )(tile_expert, tile_valid, xs, w1, b1.reshape(n_exp, 1, two_f))
    grid_spec = pltpu.PrefetchScalarGridSpec(
        num_scalar_prefetch=2,
        grid=(d // DOWN_CHUNK, nt),
        in_specs=[
            pl.BlockSpec((EXPERT_TILE, f), lambda j, t, te, tv: (t, 0)),
            pl.BlockSpec((None, None, f, DOWN_CHUNK), lambda j, t, te, tv: (layer, te[t], 0, j)),
            pl.BlockSpec((None, 1, DOWN_CHUNK), lambda j, t, te, tv: (te[t], 0, j)),
        ],
        out_specs=pl.BlockSpec((EXPERT_TILE, DOWN_CHUNK), lambda j, t, te, tv: (t, j)),
        scratch_shapes=[pltpu.VMEM((DOWN_CHUNK // LANES, f, LANES), F32), pltpu.VMEM((f, DOWN_CHUNK), BF16)],
    )
    return pl.pallas_call(
        _expert_down_kernel,
        grid_spec=grid_spec,
        out_shape=jax.ShapeDtypeStruct((n_slots, d), F32),
        compiler_params=_params(("arbitrary", "arbitrary"), 56),
        name="moe_expert_down",
    )(tile_expert, tile_valid, hidden, w2, b2.reshape(n_exp, 1, d))


def _combine_kernel(slot_ref, x_ref, tw_ref, gate_ref, fg_ref, ys_hbm, o_ref, buf, sem, *, final_norm):
    def copy(r, k):
        return pltpu.make_async_copy(ys_hbm.at[pl.ds(slot_ref[r * TOP_K + k], 1)], buf.at[k, pl.ds(r, 1)], sem)

    def issue(r, carry):
        for k in range(TOP_K):
            copy(r, k).start()
        return carry

    def drain(r, carry):
        for k in range(TOP_K):
            copy(r, k).wait()
        return carry

    lax.fori_loop(0, ROW_TILE, issue, 0)
    lax.fori_loop(0, ROW_TILE, drain, 0)
    tw = tw_ref[...]
    y = tw[:, 0:1] * buf[0]
    for k in range(1, TOP_K):
        y = y + tw[:, k:k + 1] * buf[k]
    out = x_ref[...] + gate_ref[...] * y
    if final_norm:
        out = out * lax.rsqrt(jnp.mean(out * out, axis=-1, keepdims=True) + NORM_EPS) * fg_ref[...]
    o_ref[...] = out


def moe_combine(dims, x_all, m_rows, slot_flat, top_w, gate, ys, final_g):
    d = x_all.shape[1]
    tm = ROW_TILE
    fg = jnp.ones((1, d), F32) if final_g is None else final_g.reshape(1, d)
    return pl.pallas_call(
        functools.partial(_combine_kernel, final_norm=final_g is not None),
        grid=(m_rows // tm,),
        in_specs=[
            pl.BlockSpec((tm * TOP_K,), lambda i: (i,), memory_space=pltpu.SMEM),
            pl.BlockSpec((tm, d), lambda i: (i, 0)),
            pl.BlockSpec((tm, LANES), lambda i: (i, 0)),
            pl.BlockSpec((None, 1, d), lambda i: (dims.sample_of_tile(i, tm), 0, 0)),
            pl.BlockSpec((1, d), lambda i: (0, 0)),
            pl.BlockSpec(memory_space=pl.ANY),
        ],
        out_specs=pl.BlockSpec((tm, d), lambda i: (i, 0)),
        out_shape=jax.ShapeDtypeStruct((m_rows, d), F32),
        scratch_shapes=[pltpu.VMEM((TOP_K, tm, d), F32), pltpu.SemaphoreType.DMA(())],
        compiler_params=_params(("arbitrary",), 40),
        name="moe_combine",
    )(slot_flat, x_all, top_w, gate, fg, ys)


def moe_layer(dims, x_all, m_rows, mod, norm_g, router_w, router_b, layer, w1, b1, w2, b2, final_g=None):
    n_exp = w2.shape[1]
    f_tok, top_i, top_w, pos, cnt = moe_router(dims, x_all, m_rows, norm_g, mod[3], mod[4], router_w, router_b)
    counts = cnt[0, :n_exp].astype(jnp.int32)
    tiles_e = (counts + EXPERT_TILE - 1) // EXPERT_TILE
    tile_end = jnp.cumsum(tiles_e)
    offset = (tile_end - tiles_e) * EXPERT_TILE
    slot = (offset[top_i[:, :TOP_K]] + pos[:, :TOP_K]).reshape(-1)
    nt = m_rows * TOP_K // EXPERT_TILE + n_exp
    tile_ids = jnp.arange(nt, dtype=jnp.int32)
    tile_expert = jnp.minimum(jnp.sum(tile_ids[:, None] >= tile_end[None, :], axis=1), n_exp - 1).astype(jnp.int32)
    tile_valid = (tile_ids < tile_end[-1]).astype(jnp.int32)
    xs = moe_dispatch(f_tok, slot, nt * EXPERT_TILE)
    ys = moe_experts(xs, tile_expert, tile_valid, layer, w1, b1, w2, b2)
    return moe_combine(dims, x_all, m_rows, slot, top_w, mod[5], ys, final_g)


ODD_R, ODD_K, ODD_V, ODD_WL, ODD_AL, ODD_GL, ODD_S5 = 0, 1536, 3072, 4608, 4736, 4864, 4992
ODD_COLS = 5632
RWKV_WIDTH = 1536
RWKV_HEADS = RWKV_WIDTH // RWKV_HEAD
RWKV_RANK = 64
RWKV_GROUPS = 3
RWKV_GROUP_HEADS = RWKV_HEADS // RWKV_GROUPS
RWKV_KFOLD = RWKV_HEAD // 2
SCAN_STEPS = 8


def _head_sum_matrices():
    ch = np.arange(RWKV_WIDTH)[:, None] // RWKV_HEAD
    e = (ch == np.arange(LANES)[None, :]).astype(np.float32)
    return jnp.asarray(e, BF16), jnp.asarray(e.T, BF16)


def _rwkv_prep_kernel(*refs, forward, nlb, n_lat_blocks):
    (r_ref, k_ref, v_ref, wl_ref, al_ref, rn_ref, kn_ref, vn_ref, wln_ref, aln_ref,
     mur, muk, muv, muwl, mual, w0, w2, a0, a2, kk_w, ka_w, rk_w, e_ref, et_ref) = refs[:24]
    dec_o, kk_o, b_o, k_o, wr_o, v_o, br_o, kr_o, bonus_o = refs[-9:]
    i = pl.program_id(0)
    tm = r_ref.shape[0]
    is_lat = i < n_lat_blocks
    if forward:
        edge = jnp.logical_or(jnp.logical_not(is_lat), i % nlb == 0)
    else:
        edge = jnp.logical_or(jnp.logical_not(is_lat), i % nlb == nlb - 1)
    keep = jnp.where(edge, 0.0, 1.0)

    def mix(x_ref, n_ref, mu_ref):
        x = x_ref[...]
        rows = lax.broadcasted_iota(jnp.int32, x.shape, 0)
        if forward:
            sh = jnp.where(rows == 0, n_ref[SUBLANES - 1:SUBLANES, :] * keep, pltpu.roll(x, 1, axis=0))
        else:
            sh = jnp.where(rows == tm - 1, n_ref[0:1, :] * keep, pltpu.roll(x, tm - 1, axis=0))
        return x + (sh - x) * mu_ref[...]

    e = e_ref[...]
    et = et_ref[...]
    r = mix(r_ref, rn_ref, mur)
    k = mix(k_ref, kn_ref, muk)
    v = mix(v_ref, vn_ref, muv)
    wl = mix(wl_ref, wln_ref, muwl)
    al = mix(al_ref, aln_ref, mual)
    w = -_softplus(-(w0[...] + _dot3(jnp.tanh(wl), w2[...]))) - 0.5
    decay = jnp.exp(-jnp.exp(w))
    a = _sigmoid(a0[...] + _dot3(al, a2[...]))
    kk = k * kk_w[...]
    inv = 1.0 / jnp.maximum(jnp.sqrt(_dot_rsel(kk * kk, e)), 1e-12)
    kk = kk * _dot_rsel(inv, et)
    k = k * (1.0 + (a - 1.0) * ka_w[...])
    b = kk * a
    if forward:
        order = lambda x: x
    else:
        rows = lax.broadcasted_iota(jnp.int32, (tm, tm), 0)
        cols = lax.broadcasted_iota(jnp.int32, (tm, tm), 1)
        flip = (rows + cols == tm - 1).astype(BF16)
        order = lambda x: _dot_sel(flip, x)
    dec_o[...] = order(decay)
    kk_o[...] = order(kk)
    b_o[...] = order(b)
    k_o[...] = order(k)
    wr_o[...] = order(decay * r)
    v_o[...] = order(v)
    br_o[...] = order(_dot_rsel(b * r, e))
    kr_o[...] = order(_dot_rsel(k * r, e))
    bonus_o[...] = _dot_rsel(_dot_rsel(r * k * rk_w[...], e), et) * v


def rwkv_prep(dims, proj, direction, mu, w0, w2, a0, a2, k_k, k_a, r_k, scan_arrays=None):
    wd = RWKV_WIDTH
    tm = TOKEN_BLOCK
    nblk = dims.m_all // tm
    forward = direction == 0
    per8 = tm // SUBLANES
    last8 = dims.m_all // SUBLANES - 1

    def nb(i):
        return jnp.maximum(i * per8 - 1, 0) if forward else jnp.minimum((i + 1) * per8, last8)

    def main(width, start):
        return pl.BlockSpec((tm, width), lambda i: (i, start // width))

    def neigh(width, start):
        return pl.BlockSpec((SUBLANES, width), lambda i: (nb(i), start // width))

    vec = lambda width: pl.BlockSpec((1, width), lambda i: (0, 0))
    lo = direction * RWKV_RANK
    lane_pad = lambda x: jnp.zeros((LANES,) + x.shape[1:], F32).at[lo:lo + RWKV_RANK].set(x)
    mu_d = mu[direction]
    e, et = _head_sum_matrices()
    n_lat = dims.m_lat // tm
    steps = dims.C + dims.T

    def seq_pos(i):
        is_lat = i < n_lat
        sample = jnp.where(is_lat, i // dims.nlb, i - n_lat)
        if forward:
            return 0, sample, jnp.where(is_lat, 1 + i % dims.nlb, 0), 0
        return 1, sample, jnp.where(is_lat, dims.nlb - i % dims.nlb, 0), 0

    seq_spec = lambda width: pl.BlockSpec((None, None, tm, width), seq_pos)
    wide = jax.ShapeDtypeStruct((2, dims.B, steps, wd), F32)
    lane = jax.ShapeDtypeStruct((2, dims.B, steps, LANES), F32)
    if scan_arrays is None:
        scan_arrays = [jnp.zeros(sd.shape, F32) for sd in [wide] * 6 + [lane] * 2]
    prev = list(scan_arrays)
    n_in = 24
    outs = pl.pallas_call(
        functools.partial(_rwkv_prep_kernel, forward=forward, nlb=dims.nlb, n_lat_blocks=dims.m_lat // tm),
        grid=(nblk,),
        in_specs=[main(wd, ODD_R), main(wd, ODD_K), main(wd, ODD_V), main(LANES, ODD_WL), main(LANES, ODD_AL),
                  neigh(wd, ODD_R), neigh(wd, ODD_K), neigh(wd, ODD_V), neigh(LANES, ODD_WL), neigh(LANES, ODD_AL),
                  vec(wd), vec(wd), vec(wd), vec(LANES), vec(LANES),
                  vec(wd), pl.BlockSpec((LANES, wd), lambda i: (0, 0)),
                  vec(wd), pl.BlockSpec((LANES, wd), lambda i: (0, 0)),
                  vec(wd), vec(wd), vec(wd),
                  pl.BlockSpec((wd, LANES), lambda i: (0, 0)), pl.BlockSpec((LANES, wd), lambda i: (0, 0))]
                 + [pl.BlockSpec(memory_space=pl.ANY)] * len(prev),
        out_specs=[seq_spec(wd)] * 6 + [seq_spec(LANES)] * 2 + [pl.BlockSpec((tm, wd), lambda i: (i, 0))],
        out_shape=[wide] * 6 + [lane] * 2 + [jax.ShapeDtypeStruct((dims.m_all, wd), F32)],
        input_output_aliases={n_in + n: n for n in range(len(prev))},
        compiler_params=_params(("parallel",), 56),
        name="rwkv_prep",
    )(proj, proj, proj, proj, proj, proj, proj, proj, proj, proj,
      mu_d[0:wd].reshape(1, wd), mu_d[wd:2 * wd].reshape(1, wd), mu_d[2 * wd:3 * wd].reshape(1, wd),
      lane_pad(mu_d[3 * wd:3 * wd + RWKV_RANK]).reshape(1, LANES),
      lane_pad(mu_d[3 * wd + RWKV_RANK:3 * wd + 2 * RWKV_RANK]).reshape(1, LANES),
      w0[direction].reshape(1, wd), lane_pad(w2[direction]), a0[direction].reshape(1, wd), lane_pad(a2[direction]),
      k_k.reshape(1, wd), k_a.reshape(1, wd), r_k.reshape(1, wd), e, et, *prev)
    return outs[:8], outs[8]


def _rwkv_scan_kernel(*refs):
    feats = refs[0:5]
    v_ref, br_ref, kr_ref, y_ref, st_ref, acc_ref = refs[5:11]

    @pl.when(pl.program_id(0) == 0)
    def _():
        st_ref[...] = jnp.zeros_like(st_ref)

    ngroups, nk, nv, gw = st_ref.shape
    cg = gw // 2

    def step(t, carry):
        def rows8(n, g, kb):
            return feats[n][t, pl.ds(pl.multiple_of(kb * SUBLANES, SUBLANES), SUBLANES), pl.ds(g * gw, gw)]

        def row(blk, j):
            return jnp.broadcast_to(blk[j:j + 1, :], (nv, gw))

        for g in range(ngroups):
            def reduce_k(kb, acc, g=g):
                sa, z = acc
                kk8, wr8 = rows8(1, g, kb), rows8(4, g, kb)
                for j in range(SUBLANES):
                    s = st_ref[g, kb * SUBLANES + j]
                    sa = sa - s * row(kk8, j)
                    z = z + s * row(wr8, j)
                return sa, z

            zero = jnp.zeros((nv, gw), F32)
            sa, z = lax.fori_loop(0, nk // SUBLANES, reduce_k, (zero, zero))
            acc_ref[g, 0] = sa
            acc_ref[g, 1] = z

        for g in range(ngroups):
            lanes = pl.ds(g * gw, gw)
            sa = acc_ref[g, 0]
            z = acc_ref[g, 1]
            sa = sa + pltpu.roll(sa, cg, axis=1)
            z = z + pltpu.roll(z, cg, axis=1)
            y = z + sa * br_ref[t, :, lanes] + v_ref[t, :, lanes] * kr_ref[t, :, lanes]
            acc_ref[g, 0] = sa
            y_ref[t, :, pl.ds(g * cg, cg)] = y[:, :cg]

        for g in range(ngroups):
            def update_k(kb, c, g=g):
                sa = acc_ref[g, 0]
                v2 = v_ref[t, :, pl.ds(g * gw, gw)]
                w8, b8, k8 = rows8(0, g, kb), rows8(2, g, kb), rows8(3, g, kb)
                for j in range(SUBLANES):
                    k = kb * SUBLANES + j
                    st_ref[g, k] = st_ref[g, k] * row(w8, j) + sa * row(b8, j) + v2 * row(k8, j)
                return c

            lax.fori_loop(0, nk // SUBLANES, update_k, 0)
        return carry

    lax.fori_loop(0, SCAN_STEPS, step, 0)


def rwkv_scan(feats, v, br, kr):
    steps, nk, wide = feats[0].shape
    nv = v.shape[1]
    gw = wide // RWKV_GROUPS
    nblk = steps // SCAN_STEPS
    spec = lambda rows, width: pl.BlockSpec((SCAN_STEPS, rows, width), lambda i: (i, 0, 0))
    return pl.pallas_call(
        _rwkv_scan_kernel,
        grid=(nblk,),
        in_specs=[spec(nk, wide)] * 5 + [spec(nv, wide), spec(1, wide), spec(1, wide)],
        out_specs=spec(nv, wide // 2),
        out_shape=jax.ShapeDtypeStruct((steps, nv, wide // 2), F32),
        scratch_shapes=[pltpu.VMEM((RWKV_GROUPS, nk, nv, gw), F32), pltpu.VMEM((RWKV_GROUPS, 2, nv, gw), F32)],
        compiler_params=_params(("arbitrary",), 32),
        name="rwkv_scan",
    )(*feats, v, br, kr)


def _scan_order(dims, a, forward, col_major=False):
    width = a.shape[1]
    lat = a[:dims.m_lat].reshape(dims.B, dims.T, width)
    if col_major:
        lat = lat.reshape(dims.B, dims.T // GRID_W, GRID_W, width).transpose(0, 2, 1, 3).reshape(dims.B, dims.T, width)
    ctx = a[dims.m_lat:].reshape(dims.B, dims.C, width)
    if not forward:
        lat, ctx = lat[:, ::-1], ctx[:, ::-1]
    return jnp.concatenate([ctx, lat], axis=1)


def _to_chains(dims, a, per_head, fold):
    steps = a.shape[2]
    a = a[..., :RWKV_HEADS * per_head]
    if fold:
        a = a.reshape(2, dims.B, steps, RWKV_GROUPS, RWKV_GROUP_HEADS, 2, per_head // 2)
        a = a.transpose(2, 6, 3, 5, 0, 1, 4)
        return a.reshape(steps, per_head // 2, -1)
    a = a.reshape(2, dims.B, steps, RWKV_GROUPS, 1, RWKV_GROUP_HEADS, per_head)
    a = jnp.broadcast_to(a, (2, dims.B, steps, RWKV_GROUPS, 2, RWKV_GROUP_HEADS, per_head))
    return a.transpose(2, 6, 3, 4, 0, 1, 5).reshape(steps, per_head, -1)


def _from_chains(dims, y):
    steps = y.shape[0]
    y = y.reshape(steps, RWKV_HEAD, RWKV_GROUPS, 2, dims.B, RWKV_GROUP_HEADS).transpose(3, 4, 0, 2, 5, 1)
    return y.reshape(2, dims.B, steps, RWKV_WIDTH)


def rwkv_mixer(dims, proj, mu, w0, w2, a0, a2, k_k, k_a, r_k):
    half, bonus_f = rwkv_prep(dims, proj, 0, mu, w0, w2, a0, a2, k_k, k_a, r_k)
    both, bonus_b = rwkv_prep(dims, proj, 1, mu, w0, w2, a0, a2, k_k, k_a, r_k, scan_arrays=half)
    feats = [_to_chains(dims, both[n], RWKV_HEAD, True) for n in range(5)]
    v = _to_chains(dims, both[5], RWKV_HEAD, False)
    br = _to_chains(dims, both[6], 1, False)
    kr = _to_chains(dims, both[7], 1, False)
    y = rwkv_scan(feats, v, br, kr)
    return _from_chains(dims, y), bonus_f, bonus_b


S5_WIDTH = 512
S5_GROUPS = S5_WIDTH // S5_GROUP
S5_LANES = S5_GROUPS * S5_STATE
S5_BLOCKS = S5_WIDTH // LANES
S5_STEPS = 32


def _s5_kernel(u_ref, lr_ref, li_ref, wre_ref, wim_ref, cre_ref, cim_ref, y_ref, xr_ref, xi_ref, sr_ref, si_ref,
               *, n_fwd_rows):
    @pl.when(pl.program_id(0) == 0)
    def _():
        sr_ref[...] = jnp.zeros_like(sr_ref)
        si_ref[...] = jnp.zeros_like(si_ref)

    rows = S5_STEPS * SUBLANES
    seg = S5_LANES // S5_BLOCKS
    u = u_ref[...].reshape(rows, S5_WIDTH).astype(BF16)
    backward = lax.broadcasted_iota(jnp.int32, (rows, 1), 0) % SUBLANES >= n_fwd_rows
    for q in range(S5_BLOCKS):
        uq = u[:, q * LANES:(q + 1) * LANES]
        cols = pl.ds(q * seg, seg)
        dot = functools.partial(jnp.dot, preferred_element_type=F32)
        xr_ref[:, cols] = jnp.where(backward, dot(uq, wre_ref[1, q]), dot(uq, wre_ref[0, q]))
        xi_ref[:, cols] = jnp.where(backward, dot(uq, wim_ref[1, q]), dot(uq, wim_ref[0, q]))

    lam_r = lr_ref[...]
    lam_i = li_ref[...]

    def step(t, carry):
        sr, si = carry
        r = pl.ds(pl.multiple_of(t * SUBLANES, SUBLANES), SUBLANES)
        nr = lam_r * sr - lam_i * si + xr_ref[r, :]
        ni = lam_r * si + lam_i * sr + xi_ref[r, :]
        xr_ref[r, :] = nr
        xi_ref[r, :] = ni
        return nr, ni

    sr, si = lax.fori_loop(0, S5_STEPS, step, (sr_ref[...], si_ref[...]))
    sr_ref[...] = sr
    si_ref[...] = si
    for q in range(S5_BLOCKS):
        cols = pl.ds(q * seg, seg)
        yq = _bdot(xr_ref[:, cols], cre_ref[q]) - _bdot(xi_ref[:, cols], cim_ref[q])
        y_ref[:, :, pl.ds(q * LANES, LANES)] = yq.reshape(S5_STEPS, SUBLANES, LANES)


def _s5_maps(lam_re, lam_im, log_dt, b_re, b_im, c_re, c_im):
    lre = jnp.minimum(lam_re, S5_MAX_RE)
    dt = jnp.exp(log_dt)[..., None]
    mag = jnp.exp(lre * dt)
    lbr, lbi = mag * jnp.cos(lam_im * dt), mag * jnp.sin(lam_im * dt)
    den = lre * lre + lam_im * lam_im
    cr = ((lbr - 1.0) * lre + lbi * lam_im) / den
    ci = (lbi * lre - (lbr - 1.0) * lam_im) / den
    bbr = cr[..., None] * b_re[None] - ci[..., None] * b_im[None]
    bbi = cr[..., None] * b_im[None] + ci[..., None] * b_re[None]
    gpb = S5_GROUPS // S5_BLOCKS
    eye = jnp.eye(gpb, dtype=F32)

    def in_map(bb):
        bb = bb.reshape(2, S5_BLOCKS, gpb, S5_STATE, S5_GROUP)
        return jnp.einsum("dqgpc,gh->dqgchp", bb, eye).reshape(2, S5_BLOCKS, LANES, gpb * S5_STATE).astype(BF16)

    def out_map(cc):
        cc = cc.reshape(S5_BLOCKS, gpb, S5_GROUP, S5_STATE)
        return jnp.einsum("qgcp,gh->qgphc", cc, eye).reshape(S5_BLOCKS, gpb * S5_STATE, LANES).astype(BF16)

    return lbr, lbi, in_map(bbr), in_map(bbi), out_map(c_re), out_map(c_im)


def s5_mixer(dims, u_all, lam_re, lam_im, log_dt, b_re, b_im, c_re, c_im):
    assert 2 * dims.B <= SUBLANES
    lbr, lbi, wre, wim, cre, cim = _s5_maps(lam_re, lam_im, log_dt, b_re, b_im, c_re, c_im)
    rows_dir = jnp.minimum(jnp.arange(SUBLANES) // dims.B, 1)
    lam_r = lbr.reshape(2, S5_LANES)[rows_dir]
    lam_i = lbi.reshape(2, S5_LANES)[rows_dir]
    u = jnp.stack([_scan_order(dims, u_all, True, col_major=True), _scan_order(dims, u_all, False, col_major=True)])
    steps = u.shape[2]
    u = u.reshape(2 * dims.B, steps, S5_WIDTH).transpose(1, 0, 2)
    u = jnp.pad(u, ((0, 0), (0, SUBLANES - 2 * dims.B), (0, 0)))
    full = lambda shape: pl.BlockSpec(shape, lambda i: (0,) * len(shape))
    rows = S5_STEPS * SUBLANES
    y = pl.pallas_call(
        functools.partial(_s5_kernel, n_fwd_rows=dims.B),
        grid=(steps // S5_STEPS,),
        in_specs=[pl.BlockSpec((S5_STEPS, SUBLANES, S5_WIDTH), lambda i: (i, 0, 0)),
                  full((SUBLANES, S5_LANES)), full((SUBLANES, S5_LANES)),
                  full(wre.shape), full(wim.shape), full(cre.shape), full(cim.shape)],
        out_specs=pl.BlockSpec((S5_STEPS, SUBLANES, S5_WIDTH), lambda i: (i, 0, 0)),
        out_shape=jax.ShapeDtypeStruct((steps, SUBLANES, S5_WIDTH), F32),
        scratch_shapes=[pltpu.VMEM((rows, S5_LANES), F32), pltpu.VMEM((rows, S5_LANES), F32),
                        pltpu.VMEM((SUBLANES, S5_LANES), F32), pltpu.VMEM((SUBLANES, S5_LANES), F32)],
        compiler_params=_params(("arbitrary",), 32),
        name="s5_scan",
    )(u, lam_r, lam_i, wre, wim, cre, cim)

    def latent(yd, forward):
        yd = yd[dims.C:].transpose(1, 0, 2)
        if not forward:
            yd = yd[:, ::-1]
        yd = yd.reshape(dims.B, GRID_W, dims.T // GRID_W, S5_WIDTH).transpose(0, 2, 1, 3)
        return yd.reshape(dims.m_lat, S5_WIDTH)

    return latent(y[:, :dims.B], True), latent(y[:, dims.B:2 * dims.B], False)


def _gelu_tanh(x):
    return 0.5 * x * (1.0 + jnp.tanh(math.sqrt(2.0 / math.pi) * (x + 0.044715 * x * x * x)))


def _odd_out_kernel(x_ref, yf_ref, yb_ref, bf_ref, bb_ref, gl_ref, sf_ref, sb_ref, u_ref, lng, lnb, g2_ref, dsk,
                    glw, glb, e_ref, et_ref, gate_ref, w_ref, o_ref, a_ref):
    @pl.when(pl.program_id(1) == 0)
    def _():
        e = e_ref[...]
        et = et_ref[...]
        tm = yb_ref.shape[0]
        rows = lax.broadcasted_iota(jnp.int32, (tm, tm), 0)
        cols = lax.broadcasted_iota(jnp.int32, (tm, tm), 1)
        flip = (rows + cols == tm - 1).astype(BF16)
        y = yf_ref[...] + _dot_sel(flip, yb_ref[...])
        mean = _dot_rsel(_dot_rsel(y, e) * (1.0 / RWKV_HEAD), et)
        cen = y - mean
        var = _dot_rsel(cen * cen, e) * (1.0 / RWKV_HEAD)
        yn = cen * _dot_rsel(lax.rsqrt(var + RWKV_GN_EPS), et) * lng[...] + lnb[...]
        yn = yn + bf_ref[...] + bb_ref[...]
        a_ref[:, pl.ds(0, RWKV_WIDTH)] = (yn * _dot3(_sigmoid(gl_ref[...]), g2_ref[...])).astype(BF16)
        ys = _gelu_tanh(sf_ref[...] + sb_ref[...] + dsk[...] * u_ref[...])
        a_ref[:, pl.ds(RWKV_WIDTH, S5_WIDTH)] = (ys * _sigmoid(_dot3(ys, glw[...]) + glb[...])).astype(BF16)

    m = jnp.dot(a_ref[...], w_ref[...].astype(BF16), preferred_element_type=F32)
    o_ref[...] = x_ref[...] + gate_ref[...] * m


def odd_out(dims, x_all, proj, u_all, y_scan, bonus_f, bonus_b, s_f, s_b, ln_g, ln_b, g2, d_skip, glu_w, glu_b,
            w_out, gate, tn=512):
    d = x_all.shape[1]
    tm = TOKEN_BLOCK
    wd = RWKV_WIDTH
    nlb = dims.nlb
    e, et = _head_sum_matrices()
    rowblk = lambda width: pl.BlockSpec((tm, width), lambda i, j: (i, 0))
    y_fwd = pl.BlockSpec((None, None, tm, wd), lambda i, j: (0, i // nlb, 1 + i % nlb, 0))
    y_bwd = pl.BlockSpec((None, None, tm, wd), lambda i, j: (1, i // nlb, nlb - i % nlb, 0))
    vec = lambda width: pl.BlockSpec((1, width), lambda i, j: (0, 0))
    mat = lambda r, c: pl.BlockSpec((r, c), lambda i, j: (0, 0))
    return pl.pallas_call(
        _odd_out_kernel,
        grid=(dims.m_lat // tm, d // tn),
        in_specs=[pl.BlockSpec((tm, tn), lambda i, j: (i, j)),
                  y_fwd, y_bwd, rowblk(wd), rowblk(wd),
                  pl.BlockSpec((tm, LANES), lambda i, j: (i, ODD_GL // LANES)),
                  rowblk(S5_WIDTH), rowblk(S5_WIDTH), rowblk(S5_WIDTH),
                  vec(wd), vec(wd), mat(LANES, wd), vec(S5_WIDTH), mat(S5_WIDTH, S5_WIDTH), vec(S5_WIDTH),
                  mat(wd, LANES), mat(LANES, wd),
                  pl.BlockSpec((None, 1, tn), lambda i, j: (dims.sample_of_tile(i, tm), 0, j)),
                  pl.BlockSpec((d, tn), lambda i, j: (0, j))],
        out_specs=pl.BlockSpec((tm, tn), lambda i, j: (i, j)),
        out_shape=jax.ShapeDtypeStruct((dims.m_lat, d), F32),
        scratch_shapes=[pltpu.VMEM((tm, d), BF16)],
        compiler_params=_params(("parallel", "arbitrary"), 56),
        name="odd_out",
    )(x_all, y_scan, y_scan, bonus_f, bonus_b, proj, s_f, s_b, u_all, ln_g.reshape(1, wd), ln_b.reshape(1, wd), g2,
      d_skip.reshape(1, S5_WIDTH), glu_w, glu_b.reshape(1, S5_WIDTH), e, et, gate, w_out)


def odd_mixer_layer(dims, x_all, mod, norm_g, w_in, mu, w0, w2, a0, a2, g2, k_k, k_a, r_k, ln_g, ln_b,
                    lam_re, lam_im, log_dt, b_re, b_im, c_re, c_im, d_skip, glu_w, glu_b, w_out):
    w_pad = jnp.pad(w_in, ((0, 0), (0, ODD_COLS - w_in.shape[1])))
    proj = normmod_matmul(dims, x_all, norm_g, mod[0], mod[1], w_pad, tn=512)
    y_scan, bonus_f, bonus_b = rwkv_mixer(dims, proj, mu, w0, w2, a0, a2, k_k, k_a, r_k)
    u_all = proj[:, ODD_S5:ODD_S5 + S5_WIDTH]
    s_f, s_b = s5_mixer(dims, u_all, lam_re, lam_im, log_dt, b_re, b_im, c_re, c_im)
    return odd_out(dims, x_all, proj, u_all, y_scan, bonus_f, bonus_b, s_f, s_b, ln_g, ln_b, g2, d_skip, glu_w,
                   glu_b, w_out, mod[2])


def kernel(x, c, ctx, c_ctx, ada_w, ada_b, norm1_g, norm2_g, even_w_in, gla_gate_w, gla_gate_b, gla_norm_g, conv_w, conv_b, conv_ln_g, conv_ln_b, even_w_out, odd_w_in, rwkv_mu, rwkv_w0, rwkv_w2, rwkv_a0, rwkv_a2, rwkv_g2, rwkv_k_k, rwkv_k_a, rwkv_r_k, rwkv_ln_g, rwkv_ln_b, s5_lam_re, s5_lam_im, s5_log_dt, s5_b_re, s5_b_im, s5_c_re, s5_c_im, s5_d, s5_glu_w, s5_glu_b, odd_w_out, router_w, router_b, moe_w1, moe_b1, moe_w2, moe_b2, final_g):
    B, T, D = x.shape
    C = ctx.shape[1]
    assert ada_w.shape[0] == 2, "one even layer followed by one (final) odd layer"
    dims = Dims(B, T, C)
    cvec = jnp.zeros((SUBLANES, D), F32).at[:B].set(c).at[B].set(c_ctx)
    tab = adaln_table(cvec, ada_w, ada_b)
    x_all = jnp.concatenate([x.reshape(-1, D), ctx.reshape(-1, D)], axis=0)
    mod = tab[0].reshape(SUBLANES, N_MOD, 1, D).transpose(1, 0, 2, 3)
    x_all = even_mixer_layer(dims, x_all, mod, norm1_g[0], even_w_in[0], gla_gate_w[0], gla_gate_b[0],
                             gla_norm_g[0], conv_w[0], conv_b[0], conv_ln_g[0], conv_ln_b[0], even_w_out[0])
    x_all = moe_layer(dims, x_all, dims.m_all, mod, norm2_g[0], router_w[0], router_b[0], 0, moe_w1, moe_b1[0],
                      moe_w2, moe_b2[0])
    mod = tab[1].reshape(SUBLANES, N_MOD, 1, D).transpose(1, 0, 2, 3)
    x_lat = odd_mixer_layer(dims, x_all, mod, norm1_g[1], odd_w_in[0], rwkv_mu[0], rwkv_w0[0], rwkv_w2[0],
                            rwkv_a0[0], rwkv_a2[0], rwkv_g2[0], rwkv_k_k[0], rwkv_k_a[0], rwkv_r_k[0],
                            rwkv_ln_g[0], rwkv_ln_b[0], s5_lam_re[0], s5_lam_im[0], s5_log_dt[0], s5_b_re[0],
                            s5_b_im[0], s5_c_re[0], s5_c_im[0], s5_d[0], s5_glu_w[0], s5_glu_b[0], odd_w_out[0])
    x_lat = moe_layer(dims, x_lat, dims.m_lat, mod, norm2_g[1], router_w[1], router_b[1], 1, moe_w1, moe_b1[1],
                      moe_w2, moe_b2[1], final_g=final_g)
    return x_lat.reshape(B, T, D)
```
